```python
import jax, jax.numpy as jnp
from jax import lax
import numpy as np

D_MODEL = 1024
BATCH = 16
SEQ = 2048
DEPTH = 1

MIX_WIDTH = D_MODEL
ATT_WIDTH = MIX_WIDTH // 2
ATT_HEAD_DIM = 64
ATT_HEADS = ATT_WIDTH // ATT_HEAD_DIM
MOBA_BLOCK = 256
MOBA_TOPK = 3
MOBA_QCHUNK = 32
MLSTM_WIDTH = MIX_WIDTH - ATT_WIDTH
MLSTM_HEADS = 4
MLSTM_HEAD_DIM = MLSTM_WIDTH // MLSTM_HEADS
MLSTM_CHUNK = 64
MLSTM_CONV = 4
D_FF = ((8 * D_MODEL // 3 + 127) // 128) * 128
FFN_CONV = 3
PROJ_COLS = 3 * ATT_WIDTH + 4 * MLSTM_WIDTH + 2 * MLSTM_HEADS
EPS = 1e-6

kernel_name = "hymba_moba_mlstm_convffn"


def _proj_splits():
    sizes = [ATT_WIDTH] * 3 + [MLSTM_WIDTH] * 4 + [MLSTM_HEADS] * 2
    return [int(s) for s in np.cumsum(sizes)[:-1]]


def rms_norm(x, g):
    xf = x.astype(jnp.float32)
    y = xf * lax.rsqrt(jnp.mean(xf * xf, axis=-1, keepdims=True) + EPS)
    return (y * g.astype(jnp.float32)).astype(x.dtype)


def head_rms_norm(y, n_heads, g):
    B, S, W = y.shape
    yf = y.astype(jnp.float32).reshape(B, S, n_heads, W // n_heads)
    yf = yf * lax.rsqrt(jnp.mean(yf * yf, axis=-1, keepdims=True) + EPS)
    return (yf.reshape(B, S, W) * g.astype(jnp.float32)).astype(y.dtype)


def causal_dwconv(x, w, b):
    K, C = w.shape
    y = lax.conv_general_dilated(x, w[:, None, :].astype(x.dtype), window_strides=(1,),
                                 padding=((K - 1, 0),), dimension_numbers=('NWC', 'WIO', 'NWC'),
                                 feature_group_count=C)
    return y + b.astype(x.dtype)


def moba_attention(q, k, v):
    B, S, H, Dh = q.shape
    L = MOBA_BLOCK
    nb = -(-S // L)
    pad = ((0, 0), (0, nb * L - S), (0, 0), (0, 0))

    def blocks(a):
        return jnp.pad(a.astype(jnp.float32), pad).reshape(B, nb, L, H, Dh).transpose(0, 3, 1, 2, 4)

    kb, vb = blocks(k), blocks(v)
    k_mean = jnp.mean(kb, axis=3)
    qf = q.astype(jnp.float32).transpose(0, 2, 1, 3) * (Dh ** -0.5)
    nq = S // MOBA_QCHUNK
    q_chunks = qf.reshape(B, H, nq, MOBA_QCHUNK, Dh).transpose(2, 0, 1, 3, 4)
    k_eff = min(MOBA_TOPK, nb)
    b_idx = jnp.arange(B)[:, None, None, None]
    h_idx = jnp.arange(H)[None, :, None, None]

    def one_chunk(args):
        q_c, c = args
        start = c * MOBA_QCHUNK
        q_pos = start + jnp.arange(MOBA_QCHUNK)
        blk = start // L
        gate = jnp.einsum('bhqd,bhnd->bhqn', q_c, k_mean)
        gate = jnp.where(jnp.arange(nb) < blk, gate, -jnp.inf)
        _, sel = lax.top_k(gate, k_eff)
        slot_ok = jnp.arange(k_eff) < blk
        k_sel = kb[b_idx, h_idx, sel]
        v_sel = vb[b_idx, h_idx, sel]
        k_own = lax.dynamic_index_in_dim(kb, blk, axis=2, keepdims=False)
        v_own = lax.dynamic_index_in_dim(vb, blk, axis=2, keepdims=False)
        k_pos = blk * L + jnp.arange(L)
        s_own = jnp.einsum('bhqd,bhkd->bhqk', q_c, k_own)
        s_own = jnp.where(k_pos[None, :] <= q_pos[:, None], s_own, -jnp.inf)
        s_sel = jnp.einsum('bhqd,bhqjkd->bhqjk', q_c, k_sel)
        s_sel = jnp.where(slot_ok[:, None], s_sel, -jnp.inf)
        s = jnp.concatenate([s_own, s_sel.reshape(B, H, MOBA_QCHUNK, k_eff * L)], axis=-1)
        p = jax.nn.softmax(s, axis=-1)
        p_sel = p[..., L:].reshape(B, H, MOBA_QCHUNK, k_eff, L)
        return (jnp.einsum('bhqk,bhkd->bhqd', p[..., :L], v_own)
                + jnp.einsum('bhqjk,bhqjkd->bhqd', p_sel, v_sel))

    out = lax.map(one_chunk, (q_chunks, jnp.arange(nq)))
    return out.transpose(1, 0, 3, 2, 4).reshape(B, S, H * Dh)


def mlstm_chunkwise(q, k, v, i_pre, f_pre):
    B, S, H, D = q.shape
    L = MLSTM_CHUNK
    nc = S // L
    f32 = jnp.float32

    def chunks(a):
        return a.astype(f32).reshape(B, nc, L, H, D).transpose(1, 0, 3, 2, 4)

    def gchunks(a):
        return a.reshape(B, nc, L, H).transpose(1, 0, 3, 2)

    qc = chunks(q) * (D ** -0.5)
    kc, vc = chunks(k), chunks(v)
    ic = gchunks(i_pre.astype(f32))
    lfc = gchunks(jax.nn.log_sigmoid(f_pre.astype(f32)))
    causal = jnp.tril(jnp.ones((L, L), dtype=bool))

    def step(carry, inp):
        C, n, m = carry
        q_, k_, v_, i_, lf_ = inp
        b = jnp.cumsum(lf_, axis=-1)
        g = b + m[..., None]
        dm = b[..., :, None] - b[..., None, :] + i_[..., None, :]
        dm = jnp.where(causal, dm, -jnp.inf)
        m_t = jnp.maximum(g, jnp.max(dm, axis=-1))
        w_inter = jnp.exp(g - m_t)
        w_intra = jnp.exp(dm - m_t[..., None])
        qk = jnp.einsum('bhtd,bhsd->bhts', q_, k_) * w_intra
        num = (w_inter[..., None] * jnp.einsum('bhtd,bhde->bhte', q_, C)
               + jnp.einsum('bhts,bhse->bhte', qk, v_))
        den = w_inter * jnp.einsum('bhtd,bhd->bht', q_, n) + jnp.sum(qk, axis=-1)
        h = num / jnp.maximum(jnp.abs(den), jnp.exp(-m_t))[..., None]
        decay = w_inter[..., -1]
        w_last = w_intra[..., -1, :]
        C_new = decay[..., None, None] * C + jnp.einsum('bhs,bhsd,bhse->bhde', w_last, k_, v_)
        n_new = decay[..., None] * n + jnp.einsum('bhs,bhsd->bhd', w_last, k_)
        return (C_new, n_new, m_t[..., -1]), h

    init = (jnp.zeros((B, H, D, D), f32), jnp.zeros((B, H, D), f32), jnp.zeros((B, H), f32))
    _, hs = lax.scan(step, init, (qc, kc, vc, ic, lfc))
    return hs.transpose(1, 0, 3, 2, 4).reshape(B, S, H * D)


def setup_inputs(seed: int = 0) -> dict:
    key = jax.random.key(seed)
    ks = jax.random.split(key, 18)
    f32 = jnp.float32
    nrm = lambda k, shape: jax.random.normal(k, shape, f32)
    x = nrm(ks[0], (BATCH, SEQ, D_MODEL))
    norm_mix_g = 1.0 + 0.02 * nrm(ks[1], (DEPTH, D_MODEL))
    w_in = nrm(ks[2], (DEPTH, D_MODEL, PROJ_COLS)) * D_MODEL ** -0.5
    i_bias = 0.1 * nrm(ks[3], (DEPTH, MLSTM_HEADS))
    f_bias = jnp.linspace(3.0, 6.0, MLSTM_HEADS, dtype=f32)[None, :] + 0.1 * nrm(ks[4], (DEPTH, MLSTM_HEADS))
    b_gates = jnp.concatenate([i_bias, f_bias], axis=-1)
    mlstm_conv_w = nrm(ks[5], (DEPTH, MLSTM_CONV, 2 * MLSTM_WIDTH)) * MLSTM_CONV ** -0.5
    mlstm_conv_b = 0.01 * nrm(ks[6], (DEPTH, 2 * MLSTM_WIDTH))
    att_out_g = 1.0 + 0.02 * nrm(ks[7], (DEPTH, ATT_WIDTH))
    mlstm_out_g = 1.0 + 0.02 * nrm(ks[8], (DEPTH, MLSTM_WIDTH))
    w_out = nrm(ks[9], (DEPTH, MIX_WIDTH, D_MODEL)) * MIX_WIDTH ** -0.5
    norm_ffn_g = 1.0 + 0.02 * nrm(ks[10], (DEPTH, D_MODEL))
    w_up = nrm(ks[11], (DEPTH, D_MODEL, 2 * D_FF)) * D_MODEL ** -0.5
    ffn_conv_w = nrm(ks[12], (DEPTH, FFN_CONV, 2 * D_FF)) * FFN_CONV ** -0.5
    ffn_conv_b = 0.01 * nrm(ks[13], (DEPTH, 2 * D_FF))
    w_down = nrm(ks[14], (DEPTH, D_FF, D_MODEL)) * D_FF ** -0.5
    norm_final_g = 1.0 + 0.02 * nrm(ks[15], (D_MODEL,))
    return {"x": x, "norm_mix_g": norm_mix_g, "w_in": w_in, "b_gates": b_gates,
            "mlstm_conv_w": mlstm_conv_w, "mlstm_conv_b": mlstm_conv_b, "att_out_g": att_out_g,
            "mlstm_out_g": mlstm_out_g, "w_out": w_out, "norm_ffn_g": norm_ffn_g, "w_up": w_up,
            "ffn_conv_w": ffn_conv_w, "ffn_conv_b": ffn_conv_b, "w_down": w_down,
            "norm_final_g": norm_final_g}


def reference(x, norm_mix_g, w_in, b_gates, mlstm_conv_w, mlstm_conv_b, att_out_g, mlstm_out_g,
              w_out, norm_ffn_g, w_up, ffn_conv_w, ffn_conv_b, w_down, norm_final_g):
    B, S, _ = x.shape
    splits = _proj_splits()
    h = x
    for l in range(DEPTH):
        a = rms_norm(h, norm_mix_g[l])
        u = a @ w_in[l]
        aq, ak, av, mq, mk, mv, mo, mi, mf = jnp.split(u, splits, axis=-1)
        att = moba_attention(aq.reshape(B, S, ATT_HEADS, ATT_HEAD_DIM),
                             ak.reshape(B, S, ATT_HEADS, ATT_HEAD_DIM),
                             av.reshape(B, S, ATT_HEADS, ATT_HEAD_DIM)).astype(h.dtype)
        att = head_rms_norm(att, ATT_HEADS, att_out_g[l])
        mqk = jax.nn.silu(causal_dwconv(jnp.concatenate([mq, mk], axis=-1), mlstm_conv_w[l], mlstm_conv_b[l]))
        mq, mk = jnp.split(mqk, 2, axis=-1)
        gates = jnp.concatenate([mi, mf], axis=-1) + b_gates[l].astype(h.dtype)
        mh = mlstm_chunkwise(mq.reshape(B, S, MLSTM_HEADS, MLSTM_HEAD_DIM),
                             mk.reshape(B, S, MLSTM_HEADS, MLSTM_HEAD_DIM),
                             mv.reshape(B, S, MLSTM_HEADS, MLSTM_HEAD_DIM),
                             gates[..., :MLSTM_HEADS], gates[..., MLSTM_HEADS:]).astype(h.dtype)
        mh = head_rms_norm(mh, MLSTM_HEADS, mlstm_out_g[l]) * jax.nn.sigmoid(mo)
        h = h + jnp.concatenate([att, mh], axis=-1) @ w_out[l]
        f = causal_dwconv(rms_norm(h, norm_ffn_g[l]) @ w_up[l], ffn_conv_w[l], ffn_conv_b[l])
        fg, fv = jnp.split(f, 2, axis=-1)
        h = h + (jax.nn.silu(fg) * fv) @ w_down[l]
    return rms_norm(h, norm_final_g)
```

```python
import functools

import jax
import jax.numpy as jnp
from jax import lax
from jax.experimental import pallas as pl
from jax.experimental.pallas import tpu as pltpu

EPS = 1e-6
ATT_HEADS = 8
ATT_HEAD_DIM = 64
MOBA_BLOCK = 256
MOBA_TOPK = 3
MLSTM_HEADS = 4
MLSTM_HEAD_DIM = 128
MLSTM_CONV = 4
FFN_CONV = 3
MLSTM_CHUNK = 128
HALO = 8
NEG = -1e30
FFN_COL_CHUNK = 256
VMEM_LIMIT = 56 * 1024 * 1024

F32 = jnp.float32
BF16 = jnp.bfloat16

_NT = (((1,), (1,)), ((), ()))
_TN = (((0,), (0,)), ((), ()))


def _silu(y):
    return y * (1.0 / (1.0 + jnp.exp(-y)))


def _sigmoid(y):
    return 1.0 / (1.0 + jnp.exp(-y))


def _rms(x, g):
    return x * lax.rsqrt(jnp.mean(x * x, axis=-1, keepdims=True) + EPS) * g


def _inproj_kernel(x_ref, g_ref, wqT_ref, wk_ref, wvT_ref, wm_ref, wg_ref, bg_ref, cw_ref, cb_ref,
                   qT_ref, k_ref, kmean_ref, vT_ref, mq_ref, mk_ref, mv_ref, mo_ref, gates_ref,
                   conv_ref):
    t = pl.program_id(1)
    tm = x_ref.shape[0]
    a = _rms(x_ref[...], g_ref[...]).astype(BF16)

    qT = lax.dot_general(wqT_ref[...], a, _NT, preferred_element_type=F32)
    qT_ref[...] = (qT * (ATT_HEAD_DIM ** -0.5)).astype(BF16)
    k = jnp.dot(a, wk_ref[...], preferred_element_type=F32)
    k_ref[...] = k.astype(BF16)
    kmean_ref[...] = jnp.mean(k, axis=0, keepdims=True)
    vT = lax.dot_general(wvT_ref[...], a, _NT, preferred_element_type=F32)
    vT_ref[...] = vT.astype(BF16)

    m4 = jnp.dot(a, wm_ref[...], preferred_element_type=F32)
    wm = mq_ref.shape[1]
    mv_ref[...] = m4[:, 2 * wm:3 * wm].astype(BF16)
    mo_ref[...] = m4[:, 3 * wm:4 * wm].astype(BF16)

    @pl.when(t == 0)
    def _():
        conv_ref[0:HALO, :] = jnp.zeros((HALO, 2 * wm), F32)

    conv_ref[HALO:HALO + tm, :] = m4[:, 0:2 * wm]
    y = cb_ref[...] + cw_ref[MLSTM_CONV - 1:MLSTM_CONV, :] * m4[:, 0:2 * wm]
    for j in range(MLSTM_CONV - 1):
        off = HALO - (MLSTM_CONV - 1) + j
        y = y + cw_ref[j:j + 1, :] * conv_ref[off:off + tm, :]
    conv_ref[0:HALO, :] = m4[tm - HALO:tm, 0:2 * wm]
    y = _silu(y)
    mq_ref[...] = (y[:, 0:wm] * (MLSTM_HEAD_DIM ** -0.5)).astype(BF16)
    mk_ref[...] = y[:, wm:2 * wm].astype(BF16)

    gates_ref[...] = jnp.dot(a, wg_ref[...], preferred_element_type=F32) + bg_ref[...]


def _moba_kernel(qT_ref, k_ref, vT_ref, kmh_ref, kml_ref, g_ref, o_ref, qz_ref, bias_ref, oT_ref):
    i = pl.program_id(1)
    L = MOBA_BLOCK
    dh = ATT_HEAD_DIM
    nbp = 8
    qT = qT_ref[...]

    gate = (jnp.dot(kmh_ref[...], qT, preferred_element_type=F32)
            + jnp.dot(kml_ref[...], qT, preferred_element_type=F32))
    jrow = lax.broadcasted_iota(jnp.int32, (nbp, L), 0)
    past = jrow < i
    for h in range(ATT_HEADS):
        g = gate[h * nbp:(h + 1) * nbp, :]
        gm = jnp.where(past, g, -jnp.inf)
        rank = jnp.zeros((nbp, L), jnp.int32)
        for ii in range(nbp):
            gi = gm[ii:ii + 1, :]
            beats = (gi > g) | ((gi == g) & (jrow > ii))
            rank = rank + beats.astype(jnp.int32)
        sel = past & (rank < MOBA_TOPK)
        bias_ref[h * nbp:(h + 1) * nbp, :] = jnp.where(sel, 0.0, NEG)

    zeros = jnp.zeros((dh, L), BF16)
    for h in range(ATT_HEADS):
        qh = qT[h * dh:(h + 1) * dh, :]
        lo, hi = (qh, zeros) if h % 2 == 0 else (zeros, qh)
        qz_ref[h * 2 * dh:h * 2 * dh + dh, :] = lo
        qz_ref[h * 2 * dh + dh:(h + 1) * 2 * dh, :] = hi

    krow = lax.broadcasted_iota(jnp.int32, (L, L), 0)
    qcol = lax.broadcasted_iota(jnp.int32, (L, L), 1)
    causal = krow <= qcol

    for p in range(ATT_HEADS // 2):
        lanes = slice(p * 2 * dh, (p + 1) * 2 * dh)
        heads = (2 * p, 2 * p + 1)
        k_own = k_ref[i, :, lanes]
        vT_own = vT_ref[i, lanes, :]
        init = []
        for hh, h in enumerate(heads):
            s = jnp.dot(k_own, qz_ref[h * 2 * dh:(h + 1) * 2 * dh, :], preferred_element_type=F32)
            s = jnp.where(causal, s, NEG)
            m = jnp.max(s, axis=0, keepdims=True)
            pr = jnp.exp(s - m)
            l = jnp.sum(pr, axis=0, keepdims=True)
            acc = jnp.dot(vT_own[hh * dh:(hh + 1) * dh, :], pr.astype(BF16), preferred_element_type=F32)
            init += [m, l, acc]

        def body(j, carry):
            k_j = k_ref[j, :, lanes]
            vT_j = vT_ref[j, lanes, :]
            out = []
            for hh, h in enumerate(heads):
                m, l, acc = carry[3 * hh:3 * hh + 3]
                s = jnp.dot(k_j, qz_ref[h * 2 * dh:(h + 1) * 2 * dh, :], preferred_element_type=F32)
                s = s + bias_ref[pl.ds(h * nbp + j, 1), :]
                m_new = jnp.maximum(m, jnp.max(s, axis=0, keepdims=True))
                alpha = jnp.exp(m - m_new)
                pr = jnp.exp(s - m_new)
                l = alpha * l + jnp.sum(pr, axis=0, keepdims=True)
                acc = alpha * acc + jnp.dot(vT_j[hh * dh:(hh + 1) * dh, :], pr.astype(BF16),
                                            preferred_element_type=F32)
                out += [m_new, l, acc]
            return tuple(out)

        fin = lax.fori_loop(0, i, body, tuple(init))
        for hh, h in enumerate(heads):
            _, l, acc = fin[3 * hh:3 * hh + 3]
            o = acc * (1.0 / l)
            o = o * lax.rsqrt(jnp.mean(o * o, axis=0, keepdims=True) + EPS) * g_ref[h * dh:(h + 1) * dh, :]
            oT_ref[h * dh:(h + 1) * dh, :] = o

    o_ref[...] = oT_ref[...].T.astype(BF16)


def _mlstm_kernel(q_ref, k_ref, v_ref, gates_ref, og_ref, gn_ref, out_ref, c_ref, n_ref):
    S = q_ref.shape[0]
    L = MLSTM_CHUNK
    D = MLSTM_HEAD_DIM
    H = MLSTM_HEADS
    c_ref[...] = jnp.zeros(c_ref.shape, F32)
    n_ref[...] = jnp.zeros(n_ref.shape, F32)
    trow = lax.broadcasted_iota(jnp.int32, (L, L), 0)
    scol = lax.broadcasted_iota(jnp.int32, (L, L), 1)
    causal = scol <= trow
    lower = causal.astype(F32)
    upper = (trow <= scol).astype(F32)

    def chunk(c, ms):
        r0 = pl.multiple_of(c * L, L)
        rows = pl.ds(r0, L)
        gt = gates_ref[rows, :]
        lf = jnp.minimum(gt, 0.0) - jnp.log(1.0 + jnp.exp(-jnp.abs(gt)))
        b_cols = jnp.dot(lower, lf, preferred_element_type=F32, precision=lax.Precision.HIGHEST)
        gtT = gt.T
        b_rows = jnp.dot(lf.T[0:8, :], upper, preferred_element_type=F32, precision=lax.Precision.HIGHEST)
        new_ms = []
        for h in range(H):
            cols = slice(h * D, (h + 1) * D)
            m_prev = ms[h]
            i_col = gt[:, h:h + 1]
            b_col = b_cols[:, H + h:H + h + 1]
            i_row = gtT[h:h + 1, :]
            b_row = b_rows[H + h:H + h + 1, :]
            g_col = b_col + m_prev
            dm = jnp.where(causal, b_col - b_row + i_row, NEG)
            m_t = jnp.maximum(g_col, jnp.max(dm, axis=-1, keepdims=True))
            w_inter = jnp.exp(g_col - m_t)
            w_intra = jnp.exp(dm - m_t)
            q = q_ref[rows, cols]
            k = k_ref[rows, cols]
            v = v_ref[rows, cols]
            kf = k.astype(F32)
            C = c_ref[h]
            n = n_ref[h]
            qk = lax.dot_general(q, k, _NT, preferred_element_type=F32) * w_intra
            num = (w_inter * jnp.dot(q, C.astype(BF16), preferred_element_type=F32)
                   + jnp.dot(qk.astype(BF16), v, preferred_element_type=F32))
            den = (w_inter * jnp.sum(q.astype(F32) * n, axis=-1, keepdims=True)
                   + jnp.sum(qk, axis=-1, keepdims=True))
            hid = num * (1.0 / jnp.maximum(jnp.abs(den), jnp.exp(-m_t)))
            hid = _rms(hid, gn_ref[:, cols]) * _sigmoid(og_ref[rows, cols].astype(F32))
            out_ref[rows, cols] = hid.astype(out_ref.dtype)

            m_last = m_t[L - 1:L, :]
            decay = w_inter[L - 1:L, :]
            w_last = jnp.exp(b_col[L - 1:L, :] - b_col + i_col - m_last)
            kw = kf * w_last
            c_ref[h] = decay * C + lax.dot_general(kw.astype(BF16), v, _TN, preferred_element_type=F32)
            n_ref[h] = decay * n + jnp.sum(kw, axis=0, keepdims=True)
            new_ms.append(m_last)
        return tuple(new_ms)

    lax.fori_loop(0, S // L, chunk, tuple(jnp.zeros((1, 1), F32) for _ in range(H)))


def _ffn_kernel(x_ref, att_ref, mh_ref, wout_ref, g2_ref, wup_ref, cw_ref, cb_ref, wdown_ref, g3_ref,
                out_ref, up_ref, halo_ref):
    t = pl.program_id(1)
    tm = x_ref.shape[0]
    wa = att_ref.shape[1]
    dff = wdown_ref.shape[0]
    ch = FFN_COL_CHUNK

    mix = (jnp.dot(att_ref[...], wout_ref[0:wa, :], preferred_element_type=F32)
           + jnp.dot(mh_ref[...], wout_ref[wa:, :], preferred_element_type=F32))
    h1 = x_ref[...] + mix
    a2 = _rms(h1, g2_ref[...]).astype(BF16)

    @pl.when(t == 0)
    def _():
        halo_ref[...] = jnp.zeros(halo_ref.shape, F32)

    def conv_cols(off):
        cols = slice(off, off + ch)
        up = jnp.dot(a2, wup_ref[:, cols], preferred_element_type=F32)
        up_ref[0:HALO, :] = halo_ref[:, cols]
        up_ref[HALO:HALO + tm, :] = up
        halo_ref[:, cols] = up[tm - HALO:tm, :]
        y = cb_ref[:, cols] + cw_ref[FFN_CONV - 1:FFN_CONV, cols] * up
        for j in range(FFN_CONV - 1):
            o = HALO - (FFN_CONV - 1) + j
            y = y + cw_ref[j:j + 1, cols] * up_ref[o:o + tm, :]
        return y

    acc = jnp.zeros((tm, out_ref.shape[1]), F32)
    for c in range(dff // ch):
        fg = conv_cols(c * ch)
        fv = conv_cols(dff + c * ch)
        act = (_silu(fg) * fv).astype(BF16)
        acc = acc + jnp.dot(act, wdown_ref[c * ch:(c + 1) * ch, :], preferred_element_type=F32)
    out_ref[...] = _rms(h1 + acc, g3_ref[...])


def _const_spec(shape):
    return pl.BlockSpec(shape, lambda *_: (0,) * len(shape))


def kernel(x, norm_mix_g, w_in, b_gates, mlstm_conv_w, mlstm_conv_b, att_out_g, mlstm_out_g, w_out, norm_ffn_g, w_up, ffn_conv_w, ffn_conv_b, w_down, norm_final_g):
    B, S, D = x.shape
    H, dh = ATT_HEADS, ATT_HEAD_DIM
    wa = H * dh
    wm = MLSTM_HEADS * MLSTM_HEAD_DIM
    L = MOBA_BLOCK
    nb = S // L
    assert S % L == 0 and nb <= 8 and S % MLSTM_CHUNK == 0
    assert w_in.shape[0] == 1 and w_in.shape[2] == 3 * wa + 4 * wm + 2 * MLSTM_HEADS
    dff = w_down.shape[1]
    assert dff % FFN_COL_CHUNK == 0

    wi = w_in[0]
    wqT = wi[:, 0:wa].T.astype(BF16)
    wk = wi[:, wa:2 * wa].astype(BF16)
    wvT = wi[:, 2 * wa:3 * wa].T.astype(BF16)
    wmm = wi[:, 3 * wa:3 * wa + 4 * wm].astype(BF16)
    ng = 2 * MLSTM_HEADS
    wg = jnp.pad(wi[:, 3 * wa + 4 * wm:], ((0, 0), (0, 128 - ng))).astype(BF16)
    bg = jnp.pad(b_gates[0], (0, 128 - ng)).reshape(1, 128).astype(F32)
    g1 = norm_mix_g[0].reshape(1, D)
    cw1 = mlstm_conv_w[0]
    cb1 = mlstm_conv_b[0].reshape(1, 2 * wm)

    tm = L
    nt = S // tm
    params = pltpu.CompilerParams(dimension_semantics=("arbitrary", "arbitrary"), vmem_limit_bytes=VMEM_LIMIT)
    tile = lambda w: pl.BlockSpec((None, tm, w), lambda b, t: (b, t, 0))
    qT, kb, kmean, vTb, mq, mk, mv, mo, gates = pl.pallas_call(
        _inproj_kernel,
        grid=(B, nt),
        in_specs=[tile(D), _const_spec((1, D)), _const_spec((wa, D)), _const_spec((D, wa)), _const_spec((wa, D)),
                  _const_spec((D, 4 * wm)), _const_spec((D, 128)), _const_spec((1, 128)),
                  _const_spec((MLSTM_CONV, 2 * wm)), _const_spec((1, 2 * wm))],
        out_specs=[pl.BlockSpec((None, wa, tm), lambda b, t: (b, 0, t)),
                   pl.BlockSpec((None, None, tm, wa), lambda b, t: (b, t, 0, 0)),
                   pl.BlockSpec((None, None, 1, wa), lambda b, t: (b, t, 0, 0)),
                   pl.BlockSpec((None, None, wa, tm), lambda b, t: (b, t, 0, 0)),
                   tile(wm), tile(wm), tile(wm), tile(wm), tile(128)],
        out_shape=[jax.ShapeDtypeStruct((B, wa, S), BF16),
                   jax.ShapeDtypeStruct((B, nb, L, wa), BF16),
                   jax.ShapeDtypeStruct((B, nb, 1, wa), F32),
                   jax.ShapeDtypeStruct((B, nb, wa, L), BF16),
                   jax.ShapeDtypeStruct((B, S, wm), BF16),
                   jax.ShapeDtypeStruct((B, S, wm), BF16),
                   jax.ShapeDtypeStruct((B, S, wm), BF16),
                   jax.ShapeDtypeStruct((B, S, wm), BF16),
                   jax.ShapeDtypeStruct((B, S, 128), F32)],
        scratch_shapes=[pltpu.VMEM((HALO + tm, 2 * wm), F32)],
        compiler_params=params,
    )(x, g1, wqT, wk, wvT, wmm, wg, bg, cw1, cb1)

    km = kmean.reshape(B, nb, H, dh)
    km = jnp.pad(km, ((0, 0), (0, 8 - nb), (0, 0), (0, 0)))
    eye = jnp.eye(H, dtype=F32)
    kmt = jnp.einsum('bjhd,hg->bhjgd', km, eye).reshape(B, H * 8, wa)
    kmh = kmt.astype(BF16)
    kml = (kmt - kmh.astype(F32)).astype(BF16)
    ga = att_out_g[0].reshape(wa, 1)

    att = pl.pallas_call(
        _moba_kernel,
        grid=(B, nb),
        in_specs=[pl.BlockSpec((None, wa, L), lambda b, i: (b, 0, i)),
                  pl.BlockSpec((None, nb, L, wa), lambda b, i: (b, 0, 0, 0)),
                  pl.BlockSpec((None, nb, wa, L), lambda b, i: (b, 0, 0, 0)),
                  pl.BlockSpec((None, H * 8, wa), lambda b, i: (b, 0, 0)),
                  pl.BlockSpec((None, H * 8, wa), lambda b, i: (b, 0, 0)),
                  _const_spec((wa, 1))],
        out_specs=pl.BlockSpec((None, L, wa), lambda b, i: (b, i, 0)),
        out_shape=jax.ShapeDtypeStruct((B, S, wa), BF16),
        scratch_shapes=[pltpu.VMEM((H * 2 * dh, L), BF16), pltpu.VMEM((H * 8, L), F32), pltpu.VMEM((wa, L), F32)],
        compiler_params=params,
    )(qT, kb, vTb, kmh, kml, ga)

    seq = lambda w: pl.BlockSpec((None, S, w), lambda b: (b, 0, 0))
    mh = pl.pallas_call(
        _mlstm_kernel,
        grid=(B,),
        in_specs=[seq(wm), seq(wm), seq(wm), seq(128), seq(wm), _const_spec((1, wm))],
        out_specs=seq(wm),
        out_shape=jax.ShapeDtypeStruct((B, S, wm), BF16),
        scratch_shapes=[pltpu.VMEM((MLSTM_HEADS, MLSTM_HEAD_DIM, MLSTM_HEAD_DIM), F32),
                        pltpu.VMEM((MLSTM_HEADS, 1, MLSTM_HEAD_DIM), F32)],
        compiler_params=pltpu.CompilerParams(dimension_semantics=("arbitrary",), vmem_limit_bytes=VMEM_LIMIT),
    )(mq, mk, mv, gates, mo, mlstm_out_g[0].reshape(1, wm))

    out = pl.pallas_call(
        _ffn_kernel,
        grid=(B, nt),
        in_specs=[tile(D), tile(wa), tile(wm), _const_spec((wa + wm, D)), _const_spec((1, D)),
                  _const_spec((D, 2 * dff)), _const_spec((FFN_CONV, 2 * dff)), _const_spec((1, 2 * dff)),
                  _const_spec((dff, D)), _const_spec((1, D))],
        out_specs=tile(D),
        out_shape=jax.ShapeDtypeStruct((B, S, D), x.dtype),
        scratch_shapes=[pltpu.VMEM((HALO + tm, FFN_COL_CHUNK), F32), pltpu.VMEM((HALO, 2 * dff), F32)],
        compiler_params=params,
    )(x, att, mh, w_out[0].astype(BF16), norm_ffn_g[0].reshape(1, D), w_up[0].astype(BF16), ffn_conv_w[0],
      ffn_conv_b[0].reshape(1, 2 * dff), w_down[0].astype(BF16), norm_final_g.reshape(1, D))
    return out
```

```python
import functools

import jax
import jax.numpy as jnp
from jax import lax
from jax.experimental import pallas as pl
from jax.experimental.pallas import tpu as pltpu

EPS = 1e-6
ATT_HEADS = 8
ATT_HEAD_DIM = 64
MOBA_BLOCK = 256
MOBA_TOPK = 3
MLSTM_HEADS = 4
MLSTM_HEAD_DIM = 128
MLSTM_CONV = 4
FFN_CONV = 3
MLSTM_CHUNK = 128
HALO = 8
NEG = -1e30
LOG2E = 1.4426950408889634
FFN_COL_CHUNK = 256
VMEM_LIMIT = 56 * 1024 * 1024

F32 = jnp.float32
BF16 = jnp.bfloat16

_NT = (((1,), (1,)), ((), ()))
_TN = (((0,), (0,)), ((), ()))


def _silu(y):
    return y * (1.0 / (1.0 + jnp.exp(-y)))


def _sigmoid(y):
    return 1.0 / (1.0 + jnp.exp(-y))


def _rms(x, g):
    return x * lax.rsqrt(jnp.mean(x * x, axis=-1, keepdims=True) + EPS) * g


def _inproj_kernel(x_ref, g_ref, wqT_ref, wk_ref, wvT_ref, wm_ref, wg_ref, bg_ref, cw_ref, cb_ref,
                   qT_ref, k_ref, kmean_ref, vT_ref, mq_ref, mk_ref, mv_ref, mo_ref, gates_ref,
                   conv_ref):
    t = pl.program_id(1)
    tm = x_ref.shape[0]
    a = _rms(x_ref[...], g_ref[...]).astype(BF16)

    qT = lax.dot_general(wqT_ref[...], a, _NT, preferred_element_type=F32)
    qT_ref[...] = (qT * (ATT_HEAD_DIM ** -0.5 * LOG2E)).astype(BF16)
    k = jnp.dot(a, wk_ref[...], preferred_element_type=F32)
    k_ref[...] = k.astype(BF16)
    kmean_ref[...] = jnp.mean(k, axis=0, keepdims=True)
    vT = lax.dot_general(wvT_ref[...], a, _NT, preferred_element_type=F32)
    vT_ref[...] = vT.astype(BF16)

    m4 = jnp.dot(a, wm_ref[...], preferred_element_type=F32)
    wm = mq_ref.shape[1]
    mv_ref[...] = m4[:, 2 * wm:3 * wm].astype(BF16)
    mo_ref[...] = m4[:, 3 * wm:4 * wm].astype(BF16)

    @pl.when(t == 0)
    def _():
        conv_ref[0:HALO, :] = jnp.zeros((HALO, 2 * wm), F32)

    conv_ref[HALO:HALO + tm, :] = m4[:, 0:2 * wm]
    y = cb_ref[...] + cw_ref[MLSTM_CONV - 1:MLSTM_CONV, :] * m4[:, 0:2 * wm]
    for j in range(MLSTM_CONV - 1):
        off = HALO - (MLSTM_CONV - 1) + j
        y = y + cw_ref[j:j + 1, :] * conv_ref[off:off + tm, :]
    conv_ref[0:HALO, :] = m4[tm - HALO:tm, 0:2 * wm]
    y = _silu(y)
    mq_ref[...] = (y[:, 0:wm] * (MLSTM_HEAD_DIM ** -0.5)).astype(BF16)
    mk_ref[...] = y[:, wm:2 * wm].astype(BF16)

    gates_ref[...] = jnp.dot(a, wg_ref[...], preferred_element_type=F32) + bg_ref[...]


def _moba_kernel(qT_ref, k_ref, vT_ref, kmh_ref, kml_ref, g_ref, o_ref, qz_ref, bias_ref, oT_ref, m_ref, l_ref,
                 s_ref):
    i = pl.program_id(1)
    L = MOBA_BLOCK
    dh = ATT_HEAD_DIM
    nbp = 8
    qT = qT_ref[...]

    gate = (jnp.dot(kmh_ref[...], qT, preferred_element_type=F32)
            + jnp.dot(kml_ref[...], qT, preferred_element_type=F32))
    jrow = lax.broadcasted_iota(jnp.int32, (nbp, L), 0)
    past = jrow < i
    for h in range(ATT_HEADS):
        g = gate[h * nbp:(h + 1) * nbp, :]
        gm = jnp.where(past, g, -jnp.inf)
        rank = jnp.zeros((nbp, L), jnp.int32)
        for ii in range(nbp):
            gi = gm[ii:ii + 1, :]
            beats = (gi > g) | ((gi == g) & (jrow > ii))
            rank = rank + beats.astype(jnp.int32)
        sel = past & (rank < MOBA_TOPK)
        bias_ref[h * nbp:(h + 1) * nbp, :] = jnp.where(sel, 0.0, NEG)

    zeros = jnp.zeros((dh, L), BF16)
    for h in range(ATT_HEADS):
        qh = qT[h * dh:(h + 1) * dh, :]
        lo, hi = (qh, zeros) if h % 2 == 0 else (zeros, qh)
        qz_ref[h * 2 * dh:h * 2 * dh + dh, :] = lo
        qz_ref[h * 2 * dh + dh:(h + 1) * 2 * dh, :] = hi

    krow = lax.broadcasted_iota(jnp.int32, (L, L), 0)
    qcol = lax.broadcasted_iota(jnp.int32, (L, L), 1)
    causal = krow <= qcol

    def visit(j, own):
        alphas, subs = [], []
        for p in range(ATT_HEADS // 2):
            lanes = slice(p * 2 * dh, (p + 1) * 2 * dh)
            k_j = k_ref[j, :, lanes]
            for h in (2 * p, 2 * p + 1):
                s = jnp.dot(k_j, qz_ref[h * 2 * dh:(h + 1) * 2 * dh, :], preferred_element_type=F32)
                if own:
                    s = jnp.where(causal, s, NEG)
                    m_new = jnp.max(s, axis=0, keepdims=True)
                    alphas.append(None)
                    subs.append(m_new)
                else:
                    b = bias_ref[pl.ds(h * nbp + j, 1), :]
                    m = m_ref[h:h + 1, :]
                    m_new = jnp.maximum(m, jnp.max(s, axis=0, keepdims=True) + b)
                    alphas.append(jnp.exp2(m - m_new))
                    subs.append(m_new - b)
                s_ref[h] = s
                m_ref[h:h + 1, :] = m_new
        for p in range(ATT_HEADS // 2):
            lanes = slice(p * 2 * dh, (p + 1) * 2 * dh)
            vT_j = vT_ref[j, lanes, :]
            for hh, h in enumerate((2 * p, 2 * p + 1)):
                rows = slice(h * dh, (h + 1) * dh)
                pr = jnp.exp2(s_ref[h] - subs[h])
                l = jnp.sum(pr, axis=0, keepdims=True)
                pv = jnp.dot(vT_j[hh * dh:(hh + 1) * dh, :], pr.astype(BF16), preferred_element_type=F32)
                if own:
                    l_ref[h:h + 1, :] = l
                    oT_ref[rows, :] = pv
                else:
                    l_ref[h:h + 1, :] = alphas[h] * l_ref[h:h + 1, :] + l
                    oT_ref[rows, :] = alphas[h] * oT_ref[rows, :] + pv

    visit(i, True)

    def body(j, carry):
        visit(j, False)
        return carry

    lax.fori_loop(0, i, body, 0)
    for h in range(ATT_HEADS):
        rows = slice(h * dh, (h + 1) * dh)
        o = oT_ref[rows, :] * (1.0 / l_ref[h:h + 1, :])
        oT_ref[rows, :] = o * lax.rsqrt(jnp.mean(o * o, axis=0, keepdims=True) + EPS) * g_ref[rows, :]

    o_ref[...] = oT_ref[...].T.astype(BF16)


def _mlstm_kernel(q_ref, k_ref, v_ref, gates_ref, og_ref, gn_ref, out_ref, c_ref, n_ref):
    S = q_ref.shape[0]
    L = MLSTM_CHUNK
    D = MLSTM_HEAD_DIM
    H = MLSTM_HEADS
    c_ref[...] = jnp.zeros(c_ref.shape, F32)
    n_ref[...] = jnp.zeros(n_ref.shape, F32)
    trow = lax.broadcasted_iota(jnp.int32, (L, L), 0)
    scol = lax.broadcasted_iota(jnp.int32, (L, L), 1)
    causal = scol <= trow
    lower = causal.astype(F32)
    upper = (trow <= scol).astype(F32)

    def chunk(c, ms):
        r0 = pl.multiple_of(c * L, L)
        rows = pl.ds(r0, L)
        gt = gates_ref[rows, :]
        lf = jnp.minimum(gt, 0.0) - jnp.log(1.0 + jnp.exp(-jnp.abs(gt)))
        b_cols = jnp.dot(lower, lf, preferred_element_type=F32, precision=lax.Precision.HIGHEST)
        gtT = gt.T
        b_rows = jnp.dot(lf.T[0:8, :], upper, preferred_element_type=F32, precision=lax.Precision.HIGHEST)
        new_ms = []
        for h in range(H):
            cols = slice(h * D, (h + 1) * D)
            m_prev = ms[h]
            i_col = gt[:, h:h + 1]
            b_col = b_cols[:, H + h:H + h + 1]
            i_row = gtT[h:h + 1, :]
            b_row = b_rows[H + h:H + h + 1, :]
            g_col = b_col + m_prev
            dm = jnp.where(causal, b_col - b_row + i_row, NEG)
            m_t = jnp.maximum(g_col, jnp.max(dm, axis=-1, keepdims=True))
            w_inter = jnp.exp(g_col - m_t)
            w_intra = jnp.exp(dm - m_t)
            q = q_ref[rows, cols]
            k = k_ref[rows, cols]
            v = v_ref[rows, cols]
            kf = k.astype(F32)
            C = c_ref[h]
            n = n_ref[h]
            qk = lax.dot_general(q, k, _NT, preferred_element_type=F32) * w_intra
            num = (w_inter * jnp.dot(q, C.astype(BF16), preferred_element_type=F32)
                   + jnp.dot(qk.astype(BF16), v, preferred_element_type=F32))
            den = (w_inter * jnp.sum(q.astype(F32) * n, axis=-1, keepdims=True)
                   + jnp.sum(qk, axis=-1, keepdims=True))
            hid = num * (1.0 / jnp.maximum(jnp.abs(den), jnp.exp(-m_t)))
            hid = _rms(hid, gn_ref[:, cols]) * _sigmoid(og_ref[rows, cols].astype(F32))
            out_ref[rows, cols] = hid.astype(out_ref.dtype)

            m_last = m_t[L - 1:L, :]
            decay = w_inter[L - 1:L, :]
            w_last = jnp.exp(b_col[L - 1:L, :] - b_col + i_col - m_last)
            kw = kf * w_last
            c_ref[h] = decay * C + lax.dot_general(kw.astype(BF16), v, _TN, preferred_element_type=F32)
            n_ref[h] = decay * n + jnp.sum(kw, axis=0, keepdims=True)
            new_ms.append(m_last)
        return tuple(new_ms)

    lax.fori_loop(0, S // L, chunk, tuple(jnp.zeros((1, 1), F32) for _ in range(H)))


def _ffn_kernel(x_ref, att_ref, mh_ref, wout_ref, g2_ref, wup_ref, cw_ref, cb_ref, wdown_ref, g3_ref,
                out_ref, up_ref, halo_ref):
    t = pl.program_id(1)
    tm = x_ref.shape[0]
    wa = att_ref.shape[1]
    dff = wdown_ref.shape[0]
    ch = FFN_COL_CHUNK

    mix = (jnp.dot(att_ref[...], wout_ref[0:wa, :], preferred_element_type=F32)
           + jnp.dot(mh_ref[...], wout_ref[wa:, :], preferred_element_type=F32))
    h1 = x_ref[...] + mix
    a2 = _rms(h1, g2_ref[...]).astype(BF16)

    @pl.when(t == 0)
    def _():
        halo_ref[...] = jnp.zeros(halo_ref.shape, F32)

    def conv_cols(off):
        cols = slice(off, off + ch)
        up = jnp.dot(a2, wup_ref[:, cols], preferred_element_type=F32)
        up_ref[0:HALO, :] = halo_ref[:, cols]
        up_ref[HALO:HALO + tm, :] = up
        halo_ref[:, cols] = up[tm - HALO:tm, :]
        y = cb_ref[:, cols] + cw_ref[FFN_CONV - 1:FFN_CONV, cols] * up
        for j in range(FFN_CONV - 1):
            o = HALO - (FFN_CONV - 1) + j
            y = y + cw_ref[j:j + 1, cols] * up_ref[o:o + tm, :]
        return y

    acc = jnp.zeros((tm, out_ref.shape[1]), F32)
    for c in range(dff // ch):
        fg = conv_cols(c * ch)
        fv = conv_cols(dff + c * ch)
        act = (_silu(fg) * fv).astype(BF16)
        acc = acc + jnp.dot(act, wdown_ref[c * ch:(c + 1) * ch, :], preferred_element_type=F32)
    out_ref[...] = _rms(h1 + acc, g3_ref[...])


def _const_spec(shape):
    return pl.BlockSpec(shape, lambda *_: (0,) * len(shape))


def kernel(x, norm_mix_g, w_in, b_gates, mlstm_conv_w, mlstm_conv_b, att_out_g, mlstm_out_g, w_out, norm_ffn_g, w_up, ffn_conv_w, ffn_conv_b, w_down, norm_final_g):
    B, S, D = x.shape
    H, dh = ATT_HEADS, ATT_HEAD_DIM
    wa = H * dh
    wm = MLSTM_HEADS * MLSTM_HEAD_DIM
    L = MOBA_BLOCK
    nb = S // L
    assert S % L == 0 and nb <= 8 and S % MLSTM_CHUNK == 0
    assert w_in.shape[0] == 1 and w_in.shape[2] == 3 * wa + 4 * wm + 2 * MLSTM_HEADS
    dff = w_down.shape[1]
    assert dff % FFN_COL_CHUNK == 0

    wi = w_in[0]
    wqT = wi[:, 0:wa].T.astype(BF16)
    wk = wi[:, wa:2 * wa].astype(BF16)
    wvT = wi[:, 2 * wa:3 * wa].T.astype(BF16)
    wmm = wi[:, 3 * wa:3 * wa + 4 * wm].astype(BF16)
    ng = 2 * MLSTM_HEADS
    wg = jnp.pad(wi[:, 3 * wa + 4 * wm:], ((0, 0), (0, 128 - ng))).astype(BF16)
    bg = jnp.pad(b_gates[0], (0, 128 - ng)).reshape(1, 128).astype(F32)
    g1 = norm_mix_g[0].reshape(1, D)
    cw1 = mlstm_conv_w[0]
    cb1 = mlstm_conv_b[0].reshape(1, 2 * wm)

    tm = L
    nt = S // tm
    params = pltpu.CompilerParams(dimension_semantics=("arbitrary", "arbitrary"), vmem_limit_bytes=VMEM_LIMIT)
    tile = lambda w: pl.BlockSpec((None, tm, w), lambda b, t: (b, t, 0))
    qT, kb, kmean, vTb, mq, mk, mv, mo, gates = pl.pallas_call(
        _inproj_kernel,
        grid=(B, nt),
        in_specs=[tile(D), _const_spec((1, D)), _const_spec((wa, D)), _const_spec((D, wa)), _const_spec((wa, D)),
                  _const_spec((D, 4 * wm)), _const_spec((D, 128)), _const_spec((1, 128)),
                  _const_spec((MLSTM_CONV, 2 * wm)), _const_spec((1, 2 * wm))],
        out_specs=[pl.BlockSpec((None, wa, tm), lambda b, t: (b, 0, t)),
                   pl.BlockSpec((None, None, tm, wa), lambda b, t: (b, t, 0, 0)),
                   pl.BlockSpec((None, None, 1, wa), lambda b, t: (b, t, 0, 0)),
                   pl.BlockSpec((None, None, wa, tm), lambda b, t: (b, t, 0, 0)),
                   tile(wm), tile(wm), tile(wm), tile(wm), tile(128)],
        out_shape=[jax.ShapeDtypeStruct((B, wa, S), BF16),
                   jax.ShapeDtypeStruct((B, nb, L, wa), BF16),
                   jax.ShapeDtypeStruct((B, nb, 1, wa), F32),
                   jax.ShapeDtypeStruct((B, nb, wa, L), BF16),
                   jax.ShapeDtypeStruct((B, S, wm), BF16),
                   jax.ShapeDtypeStruct((B, S, wm), BF16),
                   jax.ShapeDtypeStruct((B, S, wm), BF16),
                   jax.ShapeDtypeStruct((B, S, wm), BF16),
                   jax.ShapeDtypeStruct((B, S, 128), F32)],
        scratch_shapes=[pltpu.VMEM((HALO + tm, 2 * wm), F32)],
        compiler_params=params,
    )(x, g1, wqT, wk, wvT, wmm, wg, bg, cw1, cb1)

    km = kmean.reshape(B, nb, H, dh)
    km = jnp.pad(km, ((0, 0), (0, 8 - nb), (0, 0), (0, 0)))
    eye = jnp.eye(H, dtype=F32)
    kmt = jnp.einsum('bjhd,hg->bhjgd', km, eye).reshape(B, H * 8, wa)
    kmh = kmt.astype(BF16)
    kml = (kmt - kmh.astype(F32)).astype(BF16)
    ga = att_out_g[0].reshape(wa, 1)

    att = pl.pallas_call(
        _moba_kernel,
        grid=(B, nb),
        in_specs=[pl.BlockSpec((None, wa, L), lambda b, i: (b, 0, i)),
                  pl.BlockSpec((None, nb, L, wa), lambda b, i: (b, 0, 0, 0)),
                  pl.BlockSpec((None, nb, wa, L), lambda b, i: (b, 0, 0, 0)),
                  pl.BlockSpec((None, H * 8, wa), lambda b, i: (b, 0, 0)),
                  pl.BlockSpec((None, H * 8, wa), lambda b, i: (b, 0, 0)),
                  _const_spec((wa, 1))],
        out_specs=pl.BlockSpec((None, L, wa), lambda b, i: (b, i, 0)),
        out_shape=jax.ShapeDtypeStruct((B, S, wa), BF16),
        scratch_shapes=[pltpu.VMEM((H * 2 * dh, L), BF16), pltpu.VMEM((H * 8, L), F32), pltpu.VMEM((wa, L), F32),
                        pltpu.VMEM((H, L), F32), pltpu.VMEM((H, L), F32), pltpu.VMEM((H, L, L), F32)],
        compiler_params=params,
    )(qT, kb, vTb, kmh, kml, ga)

    seq = lambda w: pl.BlockSpec((None, S, w), lambda b: (b, 0, 0))
    mh = pl.pallas_call(
        _mlstm_kernel,
        grid=(B,),
        in_specs=[seq(wm), seq(wm), seq(wm), seq(128), seq(wm), _const_spec((1, wm))],
        out_specs=seq(wm),
        out_shape=jax.ShapeDtypeStruct((B, S, wm), BF16),
        scratch_shapes=[pltpu.VMEM((MLSTM_HEADS, MLSTM_HEAD_DIM, MLSTM_HEAD_DIM), F32),
                        pltpu.VMEM((MLSTM_HEADS, 1, MLSTM_HEAD_DIM), F32)],
        compiler_params=pltpu.CompilerParams(dimension_semantics=("arbitrary",), vmem_limit_bytes=VMEM_LIMIT),
    )(mq, mk, mv, gates, mo, mlstm_out_g[0].reshape(1, wm))

    out = pl.pallas_call(
        _ffn_kernel,
        grid=(B, nt),
        in_specs=[tile(D), tile(wa), tile(wm), _const_spec((wa + wm, D)), _const_spec((1, D)),
                  _const_spec((D, 2 * dff)), _const_spec((FFN_CONV, 2 * dff)), _const_spec((1, 2 * dff)),
                  _const_spec((dff, D)), _const_spec((1, D))],
        out_specs=tile(D),
        out_shape=jax.ShapeDtypeStruct((B, S, D), x.dtype),
        scratch_shapes=[pltpu.VMEM((HALO + tm, FFN_COL_CHUNK), F32), pltpu.VMEM((HALO, 2 * dff), F32)],
        compiler_params=params,
    )(x, att, mh, w_out[0].astype(BF16), norm_ffn_g[0].reshape(1, D), w_up[0].astype(BF16), ffn_conv_w[0],
      ffn_conv_b[0].reshape(1, 2 * dff), w_down[0].astype(BF16), norm_final_g.reshape(1, D))
    return out
```

```python
import functools

import jax
import jax.numpy as jnp
from jax import lax
from jax.experimental import pallas as pl
from jax.experimental.pallas import tpu as pltpu

EPS = 1e-6
ATT_HEADS = 8
ATT_HEAD_DIM = 64
MOBA_BLOCK = 256
MOBA_TOPK = 3
MLSTM_HEADS = 4
MLSTM_HEAD_DIM = 128
MLSTM_CONV = 4
FFN_CONV = 3
MLSTM_CHUNK = MOBA_BLOCK
INPROJ_ROW_TILE = 512
HALO = 8
NEG = -1e30
LOG2E = 1.4426950408889634
FFN_COL_CHUNK = 256
FFN_ROW_TILE = 512
FFN_STAGES = 3
VMEM_LIMIT = 56 * 1024 * 1024

F32 = jnp.float32
BF16 = jnp.bfloat16

_NT = (((1,), (1,)), ((), ()))
_TN = (((0,), (0,)), ((), ()))


def _silu(y):
    return y * (1.0 / (1.0 + jnp.exp(-y)))


def _sigmoid(y):
    return 1.0 / (1.0 + jnp.exp(-y))


def _rms(x, g):
    return x * lax.rsqrt(jnp.mean(x * x, axis=-1, keepdims=True) + EPS) * g


def _inproj_kernel(x_ref, g_ref, wn_ref, wt_ref, wgT_ref, bgT_ref, cw_ref, cb_ref,
                   qT_ref, k_ref, kmean_ref, vT_ref, mq_ref, mk_ref, mvT_ref, moT_ref, gT_ref,
                   conv_ref):
    t = pl.program_id(1)
    tm = x_ref.shape[0]
    L = MOBA_BLOCK
    wa = k_ref.shape[2]
    wm = mq_ref.shape[1]

    @pl.when(t == 0)
    def _():
        conv_ref[0:HALO, :] = jnp.zeros((HALO, 2 * wm), F32)

    a = _rms(x_ref[...], g_ref[...]).astype(BF16)
    conv_ref[HALO:HALO + tm, :] = jnp.dot(a, wn_ref[:, wa:], preferred_element_type=F32)

    k = jnp.dot(a, wn_ref[:, 0:wa], preferred_element_type=F32)
    for i in range(tm // L):
        blk = k[i * L:(i + 1) * L, :]
        k_ref[i] = blk.astype(BF16)
        kmean_ref[i] = jnp.mean(blk, axis=0, keepdims=True)

    def transposed(lo, hi):
        return lax.dot_general(wt_ref[lo:hi, :], a, _NT, preferred_element_type=F32)

    qT_ref[...] = (transposed(0, wa) * (ATT_HEAD_DIM ** -0.5 * LOG2E)).astype(BF16)
    for ref, lo, hi in ((vT_ref, wa, 2 * wa), (mvT_ref, 2 * wa, 2 * wa + wm), (moT_ref, 2 * wa + wm, 2 * wa + 2 * wm)):
        vals = transposed(lo, hi)
        for i in range(tm // L):
            ref[i] = vals[:, i * L:(i + 1) * L].astype(BF16)
    gT = lax.dot_general(wgT_ref[...], a, _NT, preferred_element_type=F32) + bgT_ref[...]
    for i in range(tm // L):
        gT_ref[i] = gT[:, i * L:(i + 1) * L]

    y = cb_ref[...]
    for j in range(MLSTM_CONV):
        off = HALO - (MLSTM_CONV - 1) + j
        y = y + cw_ref[j:j + 1, :] * conv_ref[off:off + tm, :]
    conv_ref[0:HALO, :] = conv_ref[tm:tm + HALO, :]
    y = _silu(y)
    mq_ref[...] = (y[:, 0:wm] * (MLSTM_HEAD_DIM ** -0.5)).astype(BF16)
    mk_ref[...] = y[:, wm:2 * wm].astype(BF16)


def _moba_kernel(qT_ref, k_ref, vT_ref, kmh_ref, kml_ref, g_ref, o_ref, qz_ref, bias_ref, oT_ref, m_ref, l_ref,
                 s_ref):
    i = pl.program_id(1)
    L = MOBA_BLOCK
    dh = ATT_HEAD_DIM
    nbp = 8
    qT = qT_ref[...]

    gate = (jnp.dot(kmh_ref[...], qT, preferred_element_type=F32)
            + jnp.dot(kml_ref[...], qT, preferred_element_type=F32))
    jrow = lax.broadcasted_iota(jnp.int32, (nbp, L), 0)
    past = jrow < i
    for h in range(ATT_HEADS):
        g = gate[h * nbp:(h + 1) * nbp, :]
        gm = jnp.where(past, g, -jnp.inf)
        rank = jnp.zeros((nbp, L), jnp.int32)
        for ii in range(nbp):
            gi = gm[ii:ii + 1, :]
            beats = (gi > g) | ((gi == g) & (jrow > ii))
            rank = rank + beats.astype(jnp.int32)
        sel = past & (rank < MOBA_TOPK)
        bias_ref[h * nbp:(h + 1) * nbp, :] = jnp.where(sel, 0.0, NEG)

    zeros = jnp.zeros((dh, L), BF16)
    for h in range(ATT_HEADS):
        qh = qT[h * dh:(h + 1) * dh, :]
        lo, hi = (qh, zeros) if h % 2 == 0 else (zeros, qh)
        qz_ref[h * 2 * dh:h * 2 * dh + dh, :] = lo
        qz_ref[h * 2 * dh + dh:(h + 1) * 2 * dh, :] = hi

    krow = lax.broadcasted_iota(jnp.int32, (L, L), 0)
    qcol = lax.broadcasted_iota(jnp.int32, (L, L), 1)
    causal = krow <= qcol

    def visit(j, own):
        alphas, subs = [], []
        for p in range(ATT_HEADS // 2):
            lanes = slice(p * 2 * dh, (p + 1) * 2 * dh)
            k_j = k_ref[j, :, lanes]
            for h in (2 * p, 2 * p + 1):
                s = jnp.dot(k_j, qz_ref[h * 2 * dh:(h + 1) * 2 * dh, :], preferred_element_type=F32)
                if own:
                    s = jnp.where(causal, s, NEG)
                    m_new = jnp.max(s, axis=0, keepdims=True)
                    alphas.append(None)
                    subs.append(m_new)
                else:
                    b = bias_ref[pl.ds(h * nbp + j, 1), :]
                    m = m_ref[h:h + 1, :]
                    m_new = jnp.maximum(m, jnp.max(s, axis=0, keepdims=True) + b)
                    alphas.append(jnp.exp2(m - m_new))
                    subs.append(m_new - b)
                s_ref[h] = s
                m_ref[h:h + 1, :] = m_new
        for p in range(ATT_HEADS // 2):
            lanes = slice(p * 2 * dh, (p + 1) * 2 * dh)
            vT_j = vT_ref[j, lanes, :]
            for hh, h in enumerate((2 * p, 2 * p + 1)):
                rows = slice(h * dh, (h + 1) * dh)
                pr = jnp.exp2(s_ref[h] - subs[h])
                l = jnp.sum(pr, axis=0, keepdims=True)
                pv = jnp.dot(vT_j[hh * dh:(hh + 1) * dh, :], pr.astype(BF16), preferred_element_type=F32)
                if own:
                    l_ref[h:h + 1, :] = l
                    oT_ref[rows, :] = pv
                else:
                    l_ref[h:h + 1, :] = alphas[h] * l_ref[h:h + 1, :] + l
                    oT_ref[rows, :] = alphas[h] * oT_ref[rows, :] + pv

    visit(i, True)

    def body(j, carry):
        visit(j, False)
        return carry

    lax.fori_loop(0, i, body, 0)
    for h in range(ATT_HEADS):
        rows = slice(h * dh, (h + 1) * dh)
        o = oT_ref[rows, :] * (1.0 / l_ref[h:h + 1, :])
        oT_ref[rows, :] = o * lax.rsqrt(jnp.mean(o * o, axis=0, keepdims=True) + EPS) * g_ref[rows, :]

    o_ref[...] = oT_ref[...].T.astype(BF16)


def _split3(x):
    hi = x.astype(BF16)
    r = x - hi.astype(F32)
    mid = r.astype(BF16)
    lo = (r - mid.astype(F32)).astype(BF16)
    return hi, mid, lo


def _mlstm_kernel(q_ref, k_ref, vT_ref, oT_ref, gT_ref, gn_ref, out_ref, c_ref, s_ref, hT_ref):
    nc, _, L = vT_ref.shape
    D = MLSTM_HEAD_DIM
    H = MLSTM_HEADS
    c_ref[...] = jnp.zeros(c_ref.shape, F32)
    srow = lax.broadcasted_iota(jnp.int32, (L, L), 0)
    tcol = lax.broadcasted_iota(jnp.int32, (L, L), 1)
    causal = srow <= tcol
    upper = causal.astype(BF16)
    first_row = lax.broadcasted_iota(jnp.int32, (8, L), 0) == 0

    def chunk(c, ms):
        rows = pl.ds(pl.multiple_of(c * L, L), L)
        gT = gT_ref[c]
        lfT = jnp.minimum(gT, 0.0) - jnp.log(1.0 + jnp.exp(-jnp.abs(gT)))
        bT = sum(jnp.dot(part, upper, preferred_element_type=F32) for part in _split3(lfT))
        c8 = gT[0:8, :] - bT[8:16, :]
        c_cols = jnp.concatenate([c8, jnp.zeros((128 - 8, L), F32)], axis=0).T
        inters = []
        for h in range(H):
            cols = slice(h * D, (h + 1) * D)
            q = q_ref[rows, cols]
            s_ref[h] = lax.dot_general(k_ref[rows, cols], q, _NT, preferred_element_type=F32)
            inters.append(lax.dot_general(c_ref[h].astype(BF16), q, _NT, preferred_element_type=F32))
        new_ms = []
        for h in range(H):
            cols = slice(h * D, (h + 1) * D)
            m_prev = ms[h]
            b_row = bT[8 + h:9 + h, :]
            c_row = c8[h:h + 1, :]
            cm = jnp.where(causal, c_cols[:, h:h + 1], NEG)
            mx = jnp.maximum(m_prev, jnp.max(cm, axis=0, keepdims=True))
            w_inter = jnp.exp(m_prev - mx)
            sT = s_ref[h] * jnp.exp(cm - mx)
            vT = vT_ref[c, cols, :]
            inter = inters[h]
            num = w_inter * inter[0:D, :] + jnp.dot(vT, sT.astype(BF16), preferred_element_type=F32)
            den = w_inter * inter[D:D + 1, :] + jnp.sum(sT, axis=0, keepdims=True)
            m_t = b_row + mx
            hid = num * (1.0 / jnp.maximum(jnp.abs(den), jnp.exp(-m_t)))
            hid = hid * lax.rsqrt(jnp.mean(hid * hid, axis=0, keepdims=True) + EPS) * gn_ref[cols, :]
            hT_ref[cols, :] = hid * _sigmoid(oT_ref[c, cols, :].astype(F32))

            m_last = m_t[:, L - 1:L]
            decay = w_inter[:, L - 1:L]
            w_last = jnp.exp(c_row + (b_row[:, L - 1:L] - m_last))
            lhs = jnp.concatenate([vT.astype(F32) * w_last,
                                   jnp.where(first_row, w_last, 0.0)], axis=0).astype(BF16)
            c_ref[h] = decay * c_ref[h] + jnp.dot(lhs, k_ref[rows, cols], preferred_element_type=F32)
            new_ms.append(m_last)
        out_ref[rows, :] = hT_ref[...].T.astype(out_ref.dtype)
        return tuple(new_ms)

    lax.fori_loop(0, nc, chunk, tuple(jnp.zeros((1, 1), F32) for _ in range(H)))


def _ffn_kernel(x_ref, att_ref, mh_ref, wout_ref, g2_ref, wup_ref, cw_ref, cb_ref, wdown_ref, g3_ref,
                out_ref, halo_ref, *stage_refs):
    t = pl.program_id(1)
    tm = x_ref.shape[0]
    wa = att_ref.shape[1]
    dff = wdown_ref.shape[0]
    ch = FFN_COL_CHUNK
    nst = len(stage_refs)

    mix = (jnp.dot(att_ref[...], wout_ref[0:wa, :], preferred_element_type=F32)
           + jnp.dot(mh_ref[...], wout_ref[wa:, :], preferred_element_type=F32))
    h1 = x_ref[...] + mix
    a2 = _rms(h1, g2_ref[...]).astype(BF16)

    @pl.when(t == 0)
    def _():
        halo_ref[...] = jnp.zeros(halo_ref.shape, F32)

    def project(c):
        st = stage_refs[c % nst]
        for i, off in enumerate((c * ch, dff + c * ch)):
            st[i, HALO:HALO + tm, :] = jnp.dot(a2, wup_ref[:, off:off + ch], preferred_element_type=F32)

    def gate(c):
        st = stage_refs[c % nst]
        ys = []
        for i, off in enumerate((c * ch, dff + c * ch)):
            cols = slice(off, off + ch)
            st[i, 0:HALO, :] = halo_ref[:, cols]
            y = cb_ref[:, cols]
            for j in range(FFN_CONV):
                o = HALO - (FFN_CONV - 1) + j
                y = y + cw_ref[j:j + 1, cols] * st[i, o:o + tm, :]
            halo_ref[:, cols] = st[i, tm:tm + HALO, :]
            ys.append(y)
        return (_silu(ys[0]) * ys[1]).astype(BF16)

    nch = dff // ch
    acc = jnp.zeros((tm, out_ref.shape[1]), F32)
    project(0)
    project(1)
    act = gate(0)
    for c in range(nch):
        if c + 2 < nch:
            project(c + 2)
        act_next = gate(c + 1) if c + 1 < nch else None
        acc = acc + jnp.dot(act, wdown_ref[c * ch:(c + 1) * ch, :], preferred_element_type=F32)
        act = act_next
    out_ref[...] = _rms(h1 + acc, g3_ref[...])


def _const_spec(shape):
    return pl.BlockSpec(shape, lambda *_: (0,) * len(shape))


def kernel(x, norm_mix_g, w_in, b_gates, mlstm_conv_w, mlstm_conv_b, att_out_g, mlstm_out_g, w_out, norm_ffn_g, w_up, ffn_conv_w, ffn_conv_b, w_down, norm_final_g):
    B, S, D = x.shape
    H, dh = ATT_HEADS, ATT_HEAD_DIM
    wa = H * dh
    wm = MLSTM_HEADS * MLSTM_HEAD_DIM
    L = MOBA_BLOCK
    nb = S // L
    assert S % L == 0 and nb <= 8 and MLSTM_CHUNK == L
    assert w_in.shape[0] == 1 and w_in.shape[2] == 3 * wa + 4 * wm + 2 * MLSTM_HEADS
    dff = w_down.shape[1]
    assert dff % FFN_COL_CHUNK == 0

    wi = w_in[0]
    o_mq, o_mv, o_g = 3 * wa, 3 * wa + 2 * wm, 3 * wa + 4 * wm
    wn = jnp.concatenate([wi[:, wa:2 * wa], wi[:, o_mq:o_mv]], axis=1).astype(BF16)
    wt = jnp.concatenate([wi[:, 0:wa], wi[:, 2 * wa:3 * wa], wi[:, o_mv:o_g]], axis=1).T.astype(BF16)
    nh = MLSTM_HEADS
    wgT = jnp.zeros((16, D), F32).at[0:nh].set(wi[:, o_g:o_g + nh].T).at[8:8 + nh].set(wi[:, o_g + nh:].T).astype(BF16)
    bgT = jnp.zeros((16,), F32).at[0:nh].set(b_gates[0, 0:nh]).at[8:8 + nh].set(b_gates[0, nh:])
    tm = INPROJ_ROW_TILE
    assert S % tm == 0 and tm % L == 0
    bgT = jnp.broadcast_to(bgT.reshape(16, 1), (16, tm))
    g1 = norm_mix_g[0].reshape(1, D)
    cw1 = mlstm_conv_w[0]
    cb1 = mlstm_conv_b[0].reshape(1, 2 * wm)

    nbt = tm // L
    params = pltpu.CompilerParams(dimension_semantics=("arbitrary", "arbitrary"), vmem_limit_bytes=VMEM_LIMIT)
    tile = lambda w: pl.BlockSpec((None, tm, w), lambda b, t: (b, t, 0))
    btile = lambda r, c: pl.BlockSpec((None, nbt, r, c), lambda b, t: (b, t, 0, 0))
    bshape = lambda r, c, dt: jax.ShapeDtypeStruct((B, nb, r, c), dt)
    qT, kb, kmean, vTb, mq, mk, mvT, moT, gT = pl.pallas_call(
        _inproj_kernel,
        grid=(B, S // tm),
        in_specs=[tile(D), _const_spec((1, D)), _const_spec((D, wa + 2 * wm)), _const_spec((2 * wa + 2 * wm, D)),
                  _const_spec((16, D)), _const_spec((16, tm)),
                  _const_spec((MLSTM_CONV, 2 * wm)), _const_spec((1, 2 * wm))],
        out_specs=[pl.BlockSpec((None, wa, tm), lambda b, t: (b, 0, t)),
                   btile(L, wa), btile(1, wa), btile(wa, L),
                   tile(wm), tile(wm), btile(wm, L), btile(wm, L), btile(16, L)],
        out_shape=[jax.ShapeDtypeStruct((B, wa, S), BF16),
                   bshape(L, wa, BF16), bshape(1, wa, F32), bshape(wa, L, BF16),
                   jax.ShapeDtypeStruct((B, S, wm), BF16),
                   jax.ShapeDtypeStruct((B, S, wm), BF16),
                   bshape(wm, L, BF16), bshape(wm, L, BF16), bshape(16, L, F32)],
        scratch_shapes=[pltpu.VMEM((HALO + tm, 2 * wm), F32)],
        compiler_params=params,
    )(x, g1, wn, wt, wgT, bgT, cw1, cb1)

    km = kmean.reshape(B, nb, H, dh)
    km = jnp.pad(km, ((0, 0), (0, 8 - nb), (0, 0), (0, 0)))
    eye = jnp.eye(H, dtype=F32)
    kmt = jnp.einsum('bjhd,hg->bhjgd', km, eye).reshape(B, H * 8, wa)
    kmh = kmt.astype(BF16)
    kml = (kmt - kmh.astype(F32)).astype(BF16)
    ga = att_out_g[0].reshape(wa, 1)

    att = pl.pallas_call(
        _moba_kernel,
        grid=(B, nb),
        in_specs=[pl.BlockSpec((None, wa, L), lambda b, i: (b, 0, i)),
                  pl.BlockSpec((None, nb, L, wa), lambda b, i: (b, 0, 0, 0)),
                  pl.BlockSpec((None, nb, wa, L), lambda b, i: (b, 0, 0, 0)),
                  pl.BlockSpec((None, H * 8, wa), lambda b, i: (b, 0, 0)),
                  pl.BlockSpec((None, H * 8, wa), lambda b, i: (b, 0, 0)),
                  _const_spec((wa, 1))],
        out_specs=pl.BlockSpec((None, L, wa), lambda b, i: (b, i, 0)),
        out_shape=jax.ShapeDtypeStruct((B, S, wa), BF16),
        scratch_shapes=[pltpu.VMEM((H * 2 * dh, L), BF16), pltpu.VMEM((H * 8, L), F32), pltpu.VMEM((wa, L), F32),
                        pltpu.VMEM((H, L), F32), pltpu.VMEM((H, L), F32), pltpu.VMEM((H, L, L), F32)],
        compiler_params=params,
    )(qT, kb, vTb, kmh, kml, ga)

    seq = lambda w: pl.BlockSpec((None, S, w), lambda b: (b, 0, 0))
    blocks = lambda r: pl.BlockSpec((None, nb, r, L), lambda b: (b, 0, 0, 0))
    gn = jnp.broadcast_to(mlstm_out_g[0].reshape(wm, 1), (wm, L))
    mh = pl.pallas_call(
        _mlstm_kernel,
        grid=(B,),
        in_specs=[seq(wm), seq(wm), blocks(wm), blocks(wm), blocks(16), _const_spec((wm, L))],
        out_specs=seq(wm),
        out_shape=jax.ShapeDtypeStruct((B, S, wm), BF16),
        scratch_shapes=[pltpu.VMEM((MLSTM_HEADS, MLSTM_HEAD_DIM + 8, MLSTM_HEAD_DIM), F32),
                        pltpu.VMEM((MLSTM_HEADS, L, L), F32), pltpu.VMEM((wm, L), F32)],
        compiler_params=pltpu.CompilerParams(dimension_semantics=("arbitrary",), vmem_limit_bytes=VMEM_LIMIT),
    )(mq, mk, mvT, moT, gT, gn)

    tf = FFN_ROW_TILE
    assert S % tf == 0
    ftile = lambda w: pl.BlockSpec((None, tf, w), lambda b, t: (b, t, 0))
    out = pl.pallas_call(
        _ffn_kernel,
        grid=(B, S // tf),
        in_specs=[ftile(D), ftile(wa), ftile(wm), _const_spec((wa + wm, D)), _const_spec((1, D)),
                  _const_spec((D, 2 * dff)), _const_spec((FFN_CONV, 2 * dff)), _const_spec((1, 2 * dff)),
                  _const_spec((dff, D)), _const_spec((1, D))],
        out_specs=ftile(D),
        out_shape=jax.ShapeDtypeStruct((B, S, D), x.dtype),
        scratch_shapes=[pltpu.VMEM((HALO, 2 * dff), F32)]
        + [pltpu.VMEM((2, HALO + tf, FFN_COL_CHUNK), F32) for _ in range(FFN_STAGES)],
        compiler_params=params,
    )(x, att, mh, w_out[0].astype(BF16), norm_ffn_g[0].reshape(1, D), w_up[0].astype(BF16), ffn_conv_w[0],
      ffn_conv_b[0].reshape(1, 2 * dff), w_down[0].astype(BF16), norm_final_g.reshape(1, D))
    return out
```

```python
import functools

import jax
import jax.numpy as jnp
from jax import lax
from jax.experimental import pallas as pl
from jax.experimental.pallas import tpu as pltpu

EPS = 1e-6
ATT_HEADS = 8
ATT_HEAD_DIM = 64
MOBA_BLOCK = 256
MOBA_TOPK = 3
MLSTM_HEADS = 4
MLSTM_HEAD_DIM = 128
MLSTM_CONV = 4
FFN_CONV = 3
MLSTM_CHUNK = MOBA_BLOCK
INPROJ_ROW_TILE = 512
INPROJ_CONV_CHUNK = 256
HALO = 8
NEG = -1e30
MOBA_RUN = 4
MOBA_VPAD = 16
LOG2E = 1.4426950408889634
FFN_COL_CHUNK = 256
FFN_ROW_TILE = 512
FFN_STAGES = 3
VMEM_LIMIT = 56 * 1024 * 1024

F32 = jnp.float32
BF16 = jnp.bfloat16

_NT = (((1,), (1,)), ((), ()))
_TN = (((0,), (0,)), ((), ()))


def _silu(y):
    return y * (1.0 / (1.0 + jnp.exp(-y)))


def _sigmoid(y):
    return 1.0 / (1.0 + jnp.exp(-y))


def _rms(x, g):
    return x * lax.rsqrt(jnp.mean(x * x, axis=-1, keepdims=True) + EPS) * g


def _inproj_kernel(x_ref, g_ref, wn_ref, wt_ref, wgT_ref, bgT_ref, cw_ref, cb_ref,
                   qT_ref, k_ref, kmean_ref, vT_ref, mq_ref, mk_ref, mvT_ref, moT_ref, gT_ref,
                   conv_ref):
    t = pl.program_id(1)
    tm = x_ref.shape[0]
    L = MOBA_BLOCK
    wa = k_ref.shape[2]
    wm = mq_ref.shape[1]

    @pl.when(t == 0)
    def _():
        conv_ref[0:HALO, :] = jnp.zeros((HALO, 2 * wm), F32)

    a = _rms(x_ref[...], g_ref[...]).astype(BF16)
    cc = INPROJ_CONV_CHUNK

    def project_conv(j):
        cols = slice(j * cc, (j + 1) * cc)
        conv_ref[HALO:HALO + tm, cols] = jnp.dot(a, wn_ref[:, wa + j * cc:wa + (j + 1) * cc],
                                                 preferred_element_type=F32)

    def finish_conv(j):
        cols = slice(j * cc, (j + 1) * cc)
        y = cb_ref[:, cols]
        for tap in range(MLSTM_CONV):
            off = HALO - (MLSTM_CONV - 1) + tap
            y = y + cw_ref[tap:tap + 1, cols] * conv_ref[off:off + tm, cols]
        conv_ref[0:HALO, cols] = conv_ref[tm:tm + HALO, cols]
        y = _silu(y)
        if (j + 1) * cc <= wm:
            mq_ref[:, cols] = (y * (MLSTM_HEAD_DIM ** -0.5)).astype(BF16)
        else:
            mk_ref[:, j * cc - wm:(j + 1) * cc - wm] = y.astype(BF16)

    def project_k():
        k = jnp.dot(a, wn_ref[:, 0:wa], preferred_element_type=F32)
        for i in range(tm // L):
            blk = k[i * L:(i + 1) * L, :]
            k_ref[i] = blk.astype(BF16)
            kmean_ref[i] = jnp.mean(blk, axis=0, keepdims=True)

    def transposed(lo, hi):
        return lax.dot_general(wt_ref[lo:hi, :], a, _NT, preferred_element_type=F32)

    def project_qT():
        qT_ref[...] = (transposed(0, wa) * (ATT_HEAD_DIM ** -0.5 * LOG2E)).astype(BF16)

    def project_blocks(ref, lo, hi):
        vals = transposed(lo, hi)
        for i in range(tm // L):
            ref[i] = vals[:, i * L:(i + 1) * L].astype(BF16)

    def project_vT():
        vals = transposed(wa, 2 * wa)
        dh = ATT_HEAD_DIM
        vp = dh + MOBA_VPAD
        ones_row = (lax.broadcasted_iota(jnp.int32, (MOBA_VPAD, L), 0) == 0).astype(BF16)
        for i in range(tm // L):
            for h in range(ATT_HEADS):
                vT_ref[i, h * vp:h * vp + dh, :] = vals[h * dh:(h + 1) * dh, i * L:(i + 1) * L].astype(BF16)
                vT_ref[i, h * vp + dh:(h + 1) * vp, :] = ones_row

    def project_gates():
        gT = lax.dot_general(wgT_ref[...], a, _NT, preferred_element_type=F32) + bgT_ref[...]
        for i in range(tm // L):
            gT_ref[i] = gT[:, i * L:(i + 1) * L]

    others = [project_k, project_qT, project_vT,
              lambda: project_blocks(mvT_ref, 2 * wa, 2 * wa + wm),
              lambda: project_blocks(moT_ref, 2 * wa + wm, 2 * wa + 2 * wm), project_gates]
    nconv = 2 * wm // cc
    project_conv(0)
    for j in range(nconv):
        if j + 1 < nconv:
            project_conv(j + 1)
        if others:
            others.pop(0)()
        finish_conv(j)
    for f in others:
        f()


def _moba_kernel(qT_ref, k_ref, vT_ref, kmh_ref, kml_ref, g_ref, o_ref, qz_ref, bias_ref, oT_ref, acc_ref, m_ref,
                 alpha_ref, sub_ref, s_ref):
    i = pl.program_id(1)
    L = MOBA_BLOCK
    dh = ATT_HEAD_DIM
    nbp = 8
    qT = qT_ref[...]

    gate = (jnp.dot(kmh_ref[...], qT, preferred_element_type=F32)
            + jnp.dot(kml_ref[...], qT, preferred_element_type=F32))
    jrow = lax.broadcasted_iota(jnp.int32, (nbp, L), 0)
    past = jrow < i
    for h in range(ATT_HEADS):
        g = gate[h * nbp:(h + 1) * nbp, :]
        gm = jnp.where(past, g, -jnp.inf)
        rank = jnp.zeros((nbp, L), jnp.int32)
        for ii in range(nbp):
            gi = gm[ii:ii + 1, :]
            beats = (gi > g) | ((gi == g) & (jrow > ii))
            rank = rank + beats.astype(jnp.int32)
        sel = past & (rank < MOBA_TOPK)
        bias_ref[h * nbp:(h + 1) * nbp, :] = jnp.where(sel, 0.0, NEG)

    zeros = jnp.zeros((dh, L), BF16)
    for h in range(ATT_HEADS):
        qh = qT[h * dh:(h + 1) * dh, :]
        lo, hi = (qh, zeros) if h % 2 == 0 else (zeros, qh)
        qz_ref[h * 2 * dh:h * 2 * dh + dh, :] = lo
        qz_ref[h * 2 * dh + dh:(h + 1) * 2 * dh, :] = hi

    krow = lax.broadcasted_iota(jnp.int32, (L, L), 0)
    qcol = lax.broadcasted_iota(jnp.int32, (L, L), 1)
    causal = krow <= qcol

    vp = dh + MOBA_VPAD

    def scores(j, slot, own):
        for p in range(ATT_HEADS // 2):
            k_j = k_ref[j, :, p * 2 * dh:(p + 1) * 2 * dh]
            for h in (2 * p, 2 * p + 1):
                s = jnp.dot(k_j, qz_ref[h * 2 * dh:(h + 1) * 2 * dh, :], preferred_element_type=F32)
                if own:
                    s = jnp.where(causal, s, NEG)
                    m_new = jnp.max(s, axis=0, keepdims=True)
                    alpha = jnp.zeros((1, L), F32)
                    sub = m_new
                else:
                    b = bias_ref[pl.ds(h * nbp + j, 1), :]
                    m = m_ref[h:h + 1, :]
                    m_new = jnp.maximum(m, jnp.max(s, axis=0, keepdims=True) + b)
                    alpha = jnp.exp2(m - m_new)
                    sub = m_new - b
                s_ref[slot, h] = s
                m_ref[h:h + 1, :] = m_new
                alpha_ref[slot, pl.ds(h, 1), :] = alpha
                sub_ref[slot, pl.ds(h, 1), :] = sub

    def values(j, slot):
        for h in range(ATT_HEADS):
            rows = slice(h * vp, (h + 1) * vp)
            pr = jnp.exp2(s_ref[slot, h] - sub_ref[slot, pl.ds(h, 1), :])
            pv = jnp.dot(vT_ref[j, rows, :], pr.astype(BF16), preferred_element_type=F32)
            acc_ref[rows, :] = alpha_ref[slot, pl.ds(h, 1), :] * acc_ref[rows, :] + pv

    def visit_run(blocks):
        scores(blocks[0][0], 0, blocks[0][1])
        for r, (j, _) in enumerate(blocks):
            if r + 1 < len(blocks):
                scores(blocks[r + 1][0], (r + 1) % 2, blocks[r + 1][1])
            values(j, r % 2)

    acc_ref[...] = jnp.zeros(acc_ref.shape, F32)
    tail = lax.rem(i, MOBA_RUN)
    for n in range(MOBA_RUN):
        @pl.when(tail == n)
        def _(n=n):
            visit_run([(i, True)] + [(r, False) for r in range(n)])

    def full_run(t, carry):
        visit_run([(tail + MOBA_RUN * t + r, False) for r in range(MOBA_RUN)])
        return carry

    lax.fori_loop(0, i // MOBA_RUN, full_run, 0)

    for h in range(ATT_HEADS):
        rows = slice(h * dh, (h + 1) * dh)
        acc = acc_ref[h * vp:(h + 1) * vp, :]
        o = acc[0:dh, :] * (1.0 / acc[dh:dh + 1, :])
        oT_ref[rows, :] = o * lax.rsqrt(jnp.mean(o * o, axis=0, keepdims=True) + EPS) * g_ref[rows, :]

    o_ref[...] = oT_ref[...].T.astype(BF16)


def _split3(x):
    hi = x.astype(BF16)
    r = x - hi.astype(F32)
    mid = r.astype(BF16)
    lo = (r - mid.astype(F32)).astype(BF16)
    return hi, mid, lo


def _mlstm_kernel(q_ref, k_ref, vT_ref, oT_ref, gT_ref, gn_ref, out_ref, c_ref, s_ref, hT_ref, b_ref, crow_ref,
                  ccol_ref):
    nc, _, L = vT_ref.shape
    D = MLSTM_HEAD_DIM
    H = MLSTM_HEADS
    c_ref[...] = jnp.zeros(c_ref.shape, F32)
    srow = lax.broadcasted_iota(jnp.int32, (L, L), 0)
    tcol = lax.broadcasted_iota(jnp.int32, (L, L), 1)
    causal = srow <= tcol
    upper = causal.astype(BF16)
    first_row = lax.broadcasted_iota(jnp.int32, (8, L), 0) == 0

    for c in range(nc):
        gT = gT_ref[c]
        lfT = (jnp.minimum(gT, 0.0) - jnp.log(1.0 + jnp.exp(-jnp.abs(gT)))) * LOG2E
        bT = sum(jnp.dot(part, upper, preferred_element_type=F32) for part in _split3(lfT))
        c8 = gT[0:8, :] * LOG2E - bT[8:16, :]
        b_ref[c] = bT[8:16, :]
        crow_ref[c] = c8
        ccol_ref[c] = jnp.concatenate([c8, jnp.zeros((128 - 8, L), F32)], axis=0).T

    def chunk(c, ms):
        rows = pl.ds(pl.multiple_of(c * L, L), L)
        b8 = b_ref[c]
        c8 = crow_ref[c]
        c_cols = ccol_ref[c]
        inters = []
        for h in range(H):
            cols = slice(h * D, (h + 1) * D)
            q = q_ref[rows, cols]
            s_ref[h] = lax.dot_general(k_ref[rows, cols], q, _NT, preferred_element_type=F32)
            inters.append(lax.dot_general(c_ref[h].astype(BF16), q, _NT, preferred_element_type=F32))
        new_ms = []
        for h in range(H):
            cols = slice(h * D, (h + 1) * D)
            m_prev = ms[h]
            b_row = b8[h:h + 1, :]
            c_row = c8[h:h + 1, :]
            cm = jnp.where(causal, c_cols[:, h:h + 1], NEG)
            mx = jnp.maximum(m_prev, jnp.max(cm, axis=0, keepdims=True))
            w_inter = jnp.exp2(m_prev - mx)
            sT = s_ref[h] * jnp.exp2(cm - mx)
            vT = vT_ref[c, cols, :]
            inter = inters[h]
            num = w_inter * inter[0:D, :] + jnp.dot(vT, sT.astype(BF16), preferred_element_type=F32)
            den = w_inter * inter[D:D + 1, :] + jnp.sum(sT, axis=0, keepdims=True)
            m_t = b_row + mx
            hid = num * (1.0 / jnp.maximum(jnp.abs(den), jnp.exp2(-m_t)))
            hid = hid * lax.rsqrt(jnp.mean(hid * hid, axis=0, keepdims=True) + EPS) * gn_ref[cols, :]
            hT_ref[cols, :] = hid * _sigmoid(oT_ref[c, cols, :].astype(F32))

            m_last = m_t[:, L - 1:L]
            decay = w_inter[:, L - 1:L]
            w_last = jnp.exp2(c_row + (b_row[:, L - 1:L] - m_last))
            lhs = jnp.concatenate([vT.astype(F32) * w_last,
                                   jnp.where(first_row, w_last, 0.0)], axis=0).astype(BF16)
            c_ref[h] = decay * c_ref[h] + jnp.dot(lhs, k_ref[rows, cols], preferred_element_type=F32)
            new_ms.append(m_last)
        out_ref[rows, :] = hT_ref[...].T.astype(out_ref.dtype)
        return tuple(new_ms)

    lax.fori_loop(0, nc, chunk, tuple(jnp.zeros((1, 1), F32) for _ in range(H)))


def _ffn_kernel(x_ref, att_ref, mh_ref, wout_ref, g2_ref, wup_ref, cw_ref, cb_ref, wdown_ref, g3_ref,
                out_ref, halo_ref, *stage_refs):
    t = pl.program_id(1)
    tm = x_ref.shape[0]
    wa = att_ref.shape[1]
    dff = wdown_ref.shape[0]
    ch = FFN_COL_CHUNK
    nst = len(stage_refs)

    mix = (jnp.dot(att_ref[...], wout_ref[0:wa, :], preferred_element_type=F32)
           + jnp.dot(mh_ref[...], wout_ref[wa:, :], preferred_element_type=F32))
    h1 = x_ref[...] + mix
    a2 = _rms(h1, g2_ref[...]).astype(BF16)

    @pl.when(t == 0)
    def _():
        halo_ref[...] = jnp.zeros(halo_ref.shape, F32)

    def project(c):
        st = stage_refs[c % nst]
        for i, off in enumerate((c * ch, dff + c * ch)):
            st[i, HALO:HALO + tm, :] = jnp.dot(a2, wup_ref[:, off:off + ch], preferred_element_type=F32)

    def gate(c):
        st = stage_refs[c % nst]
        ys = []
        for i, off in enumerate((c * ch, dff + c * ch)):
            cols = slice(off, off + ch)
            st[i, 0:HALO, :] = halo_ref[:, cols]
            y = cb_ref[:, cols]
            for j in range(FFN_CONV):
                o = HALO - (FFN_CONV - 1) + j
                y = y + cw_ref[j:j + 1, cols] * st[i, o:o + tm, :]
            halo_ref[:, cols] = st[i, tm:tm + HALO, :]
            ys.append(y)
        return (_silu(ys[0]) * ys[1]).astype(BF16)

    nch = dff // ch
    out_ref[...] = h1
    project(0)
    project(1)
    act = gate(0)
    for c in range(nch):
        if c + 2 < nch:
            project(c + 2)
        act_next = gate(c + 1) if c + 1 < nch else None
        out_ref[...] += jnp.dot(act, wdown_ref[c * ch:(c + 1) * ch, :], preferred_element_type=F32)
        act = act_next
    out_ref[...] = _rms(out_ref[...], g3_ref[...])


def _const_spec(shape):
    return pl.BlockSpec(shape, lambda *_: (0,) * len(shape))


def kernel(x, norm_mix_g, w_in, b_gates, mlstm_conv_w, mlstm_conv_b, att_out_g, mlstm_out_g, w_out, norm_ffn_g, w_up, ffn_conv_w, ffn_conv_b, w_down, norm_final_g):
    B, S, D = x.shape
    H, dh = ATT_HEADS, ATT_HEAD_DIM
    wa = H * dh
    vp = dh + MOBA_VPAD
    wm = MLSTM_HEADS * MLSTM_HEAD_DIM
    L = MOBA_BLOCK
    nb = S // L
    assert S % L == 0 and nb <= 8 and MLSTM_CHUNK == L
    assert w_in.shape[0] == 1 and w_in.shape[2] == 3 * wa + 4 * wm + 2 * MLSTM_HEADS
    dff = w_down.shape[1]
    assert dff % FFN_COL_CHUNK == 0

    wi = w_in[0]
    o_mq, o_mv, o_g = 3 * wa, 3 * wa + 2 * wm, 3 * wa + 4 * wm
    wn = jnp.concatenate([wi[:, wa:2 * wa], wi[:, o_mq:o_mv]], axis=1).astype(BF16)
    wt = jnp.concatenate([wi[:, 0:wa], wi[:, 2 * wa:3 * wa], wi[:, o_mv:o_g]], axis=1).T.astype(BF16)
    nh = MLSTM_HEADS
    wgT = jnp.zeros((16, D), F32).at[0:nh].set(wi[:, o_g:o_g + nh].T).at[8:8 + nh].set(wi[:, o_g + nh:].T).astype(BF16)
    bgT = jnp.zeros((16,), F32).at[0:nh].set(b_gates[0, 0:nh]).at[8:8 + nh].set(b_gates[0, nh:])
    tm = INPROJ_ROW_TILE
    assert S % tm == 0 and tm % L == 0
    bgT = jnp.broadcast_to(bgT.reshape(16, 1), (16, tm))
    g1 = norm_mix_g[0].reshape(1, D)
    cw1 = mlstm_conv_w[0]
    cb1 = mlstm_conv_b[0].reshape(1, 2 * wm)

    nbt = tm // L
    params = pltpu.CompilerParams(dimension_semantics=("arbitrary", "arbitrary"), vmem_limit_bytes=VMEM_LIMIT)
    tile = lambda w: pl.BlockSpec((None, tm, w), lambda b, t: (b, t, 0))
    btile = lambda r, c: pl.BlockSpec((None, nbt, r, c), lambda b, t: (b, t, 0, 0))
    bshape = lambda r, c, dt: jax.ShapeDtypeStruct((B, nb, r, c), dt)
    qT, kb, kmean, vTb, mq, mk, mvT, moT, gT = pl.pallas_call(
        _inproj_kernel,
        grid=(B, S // tm),
        in_specs=[tile(D), _const_spec((1, D)), _const_spec((D, wa + 2 * wm)), _const_spec((2 * wa + 2 * wm, D)),
                  _const_spec((16, D)), _const_spec((16, tm)),
                  _const_spec((MLSTM_CONV, 2 * wm)), _const_spec((1, 2 * wm))],
        out_specs=[pl.BlockSpec((None, wa, tm), lambda b, t: (b, 0, t)),
                   btile(L, wa), btile(1, wa), btile(H * vp, L),
                   tile(wm), tile(wm), btile(wm, L), btile(wm, L), btile(16, L)],
        out_shape=[jax.ShapeDtypeStruct((B, wa, S), BF16),
                   bshape(L, wa, BF16), bshape(1, wa, F32), bshape(H * vp, L, BF16),
                   jax.ShapeDtypeStruct((B, S, wm), BF16),
                   jax.ShapeDtypeStruct((B, S, wm), BF16),
                   bshape(wm, L, BF16), bshape(wm, L, BF16), bshape(16, L, F32)],
        scratch_shapes=[pltpu.VMEM((HALO + tm, 2 * wm), F32)],
        compiler_params=params,
    )(x, g1, wn, wt, wgT, bgT, cw1, cb1)

    km = kmean.reshape(B, nb, H, dh)
    km = jnp.pad(km, ((0, 0), (0, 8 - nb), (0, 0), (0, 0)))
    eye = jnp.eye(H, dtype=F32)
    kmt = jnp.einsum('bjhd,hg->bhjgd', km, eye).reshape(B, H * 8, wa)
    kmh = kmt.astype(BF16)
    kml = (kmt - kmh.astype(F32)).astype(BF16)
    ga = att_out_g[0].reshape(wa, 1)

    att = pl.pallas_call(
        _moba_kernel,
        grid=(B, nb),
        in_specs=[pl.BlockSpec((None, wa, L), lambda b, i: (b, 0, i)),
                  pl.BlockSpec((None, nb, L, wa), lambda b, i: (b, 0, 0, 0)),
                  pl.BlockSpec((None, nb, H * vp, L), lambda b, i: (b, 0, 0, 0)),
                  pl.BlockSpec((None, H * 8, wa), lambda b, i: (b, 0, 0)),
                  pl.BlockSpec((None, H * 8, wa), lambda b, i: (b, 0, 0)),
                  _const_spec((wa, 1))],
        out_specs=pl.BlockSpec((None, L, wa), lambda b, i: (b, i, 0)),
        out_shape=jax.ShapeDtypeStruct((B, S, wa), BF16),
        scratch_shapes=[pltpu.VMEM((H * 2 * dh, L), BF16), pltpu.VMEM((H * 8, L), F32), pltpu.VMEM((wa, L), F32),
                        pltpu.VMEM((H * vp, L), F32), pltpu.VMEM((H, L), F32), pltpu.VMEM((2, H, L), F32),
                        pltpu.VMEM((2, H, L), F32), pltpu.VMEM((2, H, L, L), F32)],
        compiler_params=params,
    )(qT, kb, vTb, kmh, kml, ga)

    seq = lambda w: pl.BlockSpec((None, S, w), lambda b: (b, 0, 0))
    blocks = lambda r: pl.BlockSpec((None, nb, r, L), lambda b: (b, 0, 0, 0))
    gn = jnp.broadcast_to(mlstm_out_g[0].reshape(wm, 1), (wm, L))
    mh = pl.pallas_call(
        _mlstm_kernel,
        grid=(B,),
        in_specs=[seq(wm), seq(wm), blocks(wm), blocks(wm), blocks(16), _const_spec((wm, L))],
        out_specs=seq(wm),
        out_shape=jax.ShapeDtypeStruct((B, S, wm), BF16),
        scratch_shapes=[pltpu.VMEM((MLSTM_HEADS, MLSTM_HEAD_DIM + 8, MLSTM_HEAD_DIM), F32),
                        pltpu.VMEM((MLSTM_HEADS, L, L), F32), pltpu.VMEM((wm, L), F32),
                        pltpu.VMEM((nb, 8, L), F32), pltpu.VMEM((nb, 8, L), F32), pltpu.VMEM((nb, L, 128), F32)],
        compiler_params=pltpu.CompilerParams(dimension_semantics=("arbitrary",), vmem_limit_bytes=VMEM_LIMIT),
    )(mq, mk, mvT, moT, gT, gn)

    tf = FFN_ROW_TILE
    assert S % tf == 0
    ftile = lambda w: pl.BlockSpec((None, tf, w), lambda b, t: (b, t, 0))
    out = pl.pallas_call(
        _ffn_kernel,
        grid=(B, S // tf),
        in_specs=[ftile(D), ftile(wa), ftile(wm), _const_spec((wa + wm, D)), _const_spec((1, D)),
                  _const_spec((D, 2 * dff)), _const_spec((FFN_CONV, 2 * dff)), _const_spec((1, 2 * dff)),
                  _const_spec((dff, D)), _const_spec((1, D))],
        out_specs=ftile(D),
        out_shape=jax.ShapeDtypeStruct((B, S, D), x.dtype),
        scratch_shapes=[pltpu.VMEM((HALO, 2 * dff), F32)]
        + [pltpu.VMEM((2, HALO + tf, FFN_COL_CHUNK), F32) for _ in range(FFN_STAGES)],
        compiler_params=params,
    )(x, att, mh, w_out[0].astype(BF16), norm_ffn_g[0].reshape(1, D), w_up[0].astype(BF16), ffn_conv_w[0],
      ffn_conv_b[0].reshape(1, 2 * dff), w_down[0].astype(BF16), norm_final_g.reshape(1, D))
    return out
```

```python
import functools

import jax
import jax.numpy as jnp
from jax import lax
from jax.experimental import pallas as pl
from jax.experimental.pallas import tpu as pltpu

EPS = 1e-6
ATT_HEADS = 8
ATT_HEAD_DIM = 64
MOBA_BLOCK = 256
MOBA_TOPK = 3
MLSTM_HEADS = 4
MLSTM_HEAD_DIM = 128
MLSTM_CONV = 4
FFN_CONV = 3
MLSTM_CHUNK = MOBA_BLOCK
INPROJ_ROW_TILE = 512
INPROJ_CONV_CHUNK = 256
HALO = 8
NEG = -1e30
MOBA_RUN = 4
MOBA_VPAD = 16
LOG2E = 1.4426950408889634
FFN_COL_CHUNK = 256
FFN_ROW_TILE = 512
FFN_STAGES = 3
FFN_ROW_PARTS = 4
VMEM_LIMIT = 56 * 1024 * 1024

F32 = jnp.float32
BF16 = jnp.bfloat16

_NT = (((1,), (1,)), ((), ()))
_TN = (((0,), (0,)), ((), ()))


def _silu(y):
    return y * (1.0 / (1.0 + jnp.exp(-y)))


def _sigmoid(y):
    return 1.0 / (1.0 + jnp.exp(-y))


def _rms(x, g):
    return x * lax.rsqrt(jnp.mean(x * x, axis=-1, keepdims=True) + EPS) * g


def _inproj_kernel(x_ref, g_ref, wn_ref, wt_ref, wgT_ref, bgT_ref, cw_ref, cb_ref,
                   qT_ref, k_ref, kmean_ref, vT_ref, mq_ref, mk_ref, mvT_ref, moT_ref, gT_ref,
                   conv_ref):
    t = pl.program_id(1)
    tm = x_ref.shape[0]
    L = MOBA_BLOCK
    wa = k_ref.shape[2]
    wm = mq_ref.shape[1]

    @pl.when(t == 0)
    def _():
        conv_ref[0:HALO, :] = jnp.zeros((HALO, 2 * wm), F32)

    a = _rms(x_ref[...], g_ref[...]).astype(BF16)
    cc = INPROJ_CONV_CHUNK

    def project_conv(j):
        cols = slice(j * cc, (j + 1) * cc)
        conv_ref[HALO:HALO + tm, cols] = jnp.dot(a, wn_ref[:, wa + j * cc:wa + (j + 1) * cc],
                                                 preferred_element_type=F32)

    def finish_conv(j):
        cols = slice(j * cc, (j + 1) * cc)
        y = cb_ref[:, cols]
        for tap in range(MLSTM_CONV):
            off = HALO - (MLSTM_CONV - 1) + tap
            y = y + cw_ref[tap:tap + 1, cols] * conv_ref[off:off + tm, cols]
        conv_ref[0:HALO, cols] = conv_ref[tm:tm + HALO, cols]
        y = _silu(y)
        if (j + 1) * cc <= wm:
            mq_ref[:, cols] = (y * (MLSTM_HEAD_DIM ** -0.5)).astype(BF16)
        else:
            mk_ref[:, j * cc - wm:(j + 1) * cc - wm] = y.astype(BF16)

    def project_k():
        k = jnp.dot(a, wn_ref[:, 0:wa], preferred_element_type=F32)
        for i in range(tm // L):
            blk = k[i * L:(i + 1) * L, :]
            k_ref[i] = blk.astype(BF16)
            kmean_ref[i] = jnp.mean(blk, axis=0, keepdims=True)

    def transposed(lo, hi):
        return lax.dot_general(wt_ref[lo:hi, :], a, _NT, preferred_element_type=F32)

    def project_qT():
        qT_ref[...] = (transposed(0, wa) * (ATT_HEAD_DIM ** -0.5 * LOG2E)).astype(BF16)

    def project_blocks(ref, lo, hi):
        vals = transposed(lo, hi)
        for i in range(tm // L):
            ref[i] = vals[:, i * L:(i + 1) * L].astype(BF16)

    def project_vT():
        vals = transposed(wa, 2 * wa)
        dh = ATT_HEAD_DIM
        vp = dh + MOBA_VPAD
        ones_row = (lax.broadcasted_iota(jnp.int32, (MOBA_VPAD, L), 0) == 0).astype(BF16)
        for i in range(tm // L):
            for h in range(ATT_HEADS):
                vT_ref[i, h * vp:h * vp + dh, :] = vals[h * dh:(h + 1) * dh, i * L:(i + 1) * L].astype(BF16)
                vT_ref[i, h * vp + dh:(h + 1) * vp, :] = ones_row

    def project_gates():
        gT = lax.dot_general(wgT_ref[...], a, _NT, preferred_element_type=F32) + bgT_ref[...]
        for i in range(tm // L):
            gT_ref[i] = gT[:, i * L:(i + 1) * L]

    others = [project_k, project_qT, project_vT,
              lambda: project_blocks(mvT_ref, 2 * wa, 2 * wa + wm),
              lambda: project_blocks(moT_ref, 2 * wa + wm, 2 * wa + 2 * wm), project_gates]
    nconv = 2 * wm // cc
    project_conv(0)
    for j in range(nconv):
        if j + 1 < nconv:
            project_conv(j + 1)
        if others:
            others.pop(0)()
        finish_conv(j)
    for f in others:
        f()


def _moba_kernel(qT_ref, k_ref, vT_ref, kmh_ref, kml_ref, g_ref, o_ref, qz_ref, bias_ref, oT_ref, acc_ref, m_ref,
                 alpha_ref, sub_ref, s_ref):
    i = pl.program_id(1)
    L = MOBA_BLOCK
    dh = ATT_HEAD_DIM
    nbp = 8
    def prepare():
        qT = qT_ref[...]
        zeros = jnp.zeros((dh, L), BF16)
        for h in range(ATT_HEADS):
            qh = qT[h * dh:(h + 1) * dh, :]
            lo, hi = (qh, zeros) if h % 2 == 0 else (zeros, qh)
            qz_ref[h * 2 * dh:h * 2 * dh + dh, :] = lo
            qz_ref[h * 2 * dh + dh:(h + 1) * 2 * dh, :] = hi
        gate = (jnp.dot(kmh_ref[...], qT, preferred_element_type=F32)
                + jnp.dot(kml_ref[...], qT, preferred_element_type=F32))
        jrow = lax.broadcasted_iota(jnp.int32, (nbp, L), 0)
        past = jrow < i
        for h in range(ATT_HEADS):
            g = gate[h * nbp:(h + 1) * nbp, :]
            gm = jnp.where(past, g, -jnp.inf)
            rank = jnp.zeros((nbp, L), jnp.int32)
            for ii in range(nbp):
                gi = gm[ii:ii + 1, :]
                beats = jnp.where(jrow > ii, jnp.where(gi >= g, 1, 0), jnp.where(gi > g, 1, 0))
                rank = rank + beats
            sel = past & (rank < MOBA_TOPK)
            bias_ref[h * nbp:(h + 1) * nbp, :] = jnp.where(sel, 0.0, NEG)

    krow = lax.broadcasted_iota(jnp.int32, (L, L), 0)
    qcol = lax.broadcasted_iota(jnp.int32, (L, L), 1)
    causal = krow <= qcol

    vp = dh + MOBA_VPAD

    def scores(j, slot, own):
        for p in range(ATT_HEADS // 2):
            k_j = k_ref[j, :, p * 2 * dh:(p + 1) * 2 * dh]
            for h in (2 * p, 2 * p + 1):
                s = jnp.dot(k_j, qz_ref[h * 2 * dh:(h + 1) * 2 * dh, :], preferred_element_type=F32)
                if own:
                    s = jnp.where(causal, s, NEG)
                    m_new = jnp.max(s, axis=0, keepdims=True)
                    alpha = jnp.zeros((1, L), F32)
                    sub = m_new
                else:
                    b = bias_ref[pl.ds(h * nbp + j, 1), :]
                    m = m_ref[h:h + 1, :]
                    m_new = jnp.maximum(m, jnp.max(s, axis=0, keepdims=True) + b)
                    alpha = jnp.exp2(m - m_new)
                    sub = m_new - b
                s_ref[slot, h] = s
                m_ref[h:h + 1, :] = m_new
                alpha_ref[slot, pl.ds(h, 1), :] = alpha
                sub_ref[slot, pl.ds(h, 1), :] = sub

    def values(j, slot):
        for h in range(ATT_HEADS):
            rows = slice(h * vp, (h + 1) * vp)
            pr = jnp.exp2(s_ref[slot, h] - sub_ref[slot, pl.ds(h, 1), :])
            pv = jnp.dot(vT_ref[j, rows, :], pr.astype(BF16), preferred_element_type=F32)
            acc_ref[rows, :] = alpha_ref[slot, pl.ds(h, 1), :] * acc_ref[rows, :] + pv

    def visit_run(blocks):
        scores(blocks[0][0], 0, blocks[0][1])
        for r, (j, _) in enumerate(blocks):
            if r + 1 < len(blocks):
                scores(blocks[r + 1][0], (r + 1) % 2, blocks[r + 1][1])
            values(j, r % 2)

    acc_ref[...] = jnp.zeros(acc_ref.shape, F32)
    tail = lax.rem(i, MOBA_RUN)
    for n in range(MOBA_RUN):
        @pl.when(tail == n)
        def _(n=n):
            prepare()
            visit_run([(i, True)] + [(r, False) for r in range(n)])

    def full_run(t, carry):
        visit_run([(tail + MOBA_RUN * t + r, False) for r in range(MOBA_RUN)])
        return carry

    lax.fori_loop(0, i // MOBA_RUN, full_run, 0)

    for h in range(ATT_HEADS):
        rows = slice(h * dh, (h + 1) * dh)
        acc = acc_ref[h * vp:(h + 1) * vp, :]
        o = acc[0:dh, :] * (1.0 / acc[dh:dh + 1, :])
        oT_ref[rows, :] = o * lax.rsqrt(jnp.mean(o * o, axis=0, keepdims=True) + EPS) * g_ref[rows, :]

    o_ref[...] = oT_ref[...].T.astype(BF16)


def _split3(x):
    hi = x.astype(BF16)
    r = x - hi.astype(F32)
    mid = r.astype(BF16)
    lo = (r - mid.astype(F32)).astype(BF16)
    return hi, mid, lo


def _mlstm_kernel(q_ref, k_ref, vT_ref, oT_ref, gT_ref, gn_ref, out_ref, c_ref, s_ref, hT_ref, b_ref, crow_ref,
                  ccol_ref):
    nc, _, L = vT_ref.shape
    D = MLSTM_HEAD_DIM
    H = MLSTM_HEADS
    c_ref[...] = jnp.zeros(c_ref.shape, F32)
    srow = lax.broadcasted_iota(jnp.int32, (L, L), 0)
    tcol = lax.broadcasted_iota(jnp.int32, (L, L), 1)
    causal = srow <= tcol
    upper = causal.astype(BF16)
    first_row = lax.broadcasted_iota(jnp.int32, (8, L), 0) == 0

    for c in range(nc):
        gT = gT_ref[c]
        lfT = (jnp.minimum(gT, 0.0) - jnp.log(1.0 + jnp.exp(-jnp.abs(gT)))) * LOG2E
        bT = sum(jnp.dot(part, upper, preferred_element_type=F32) for part in _split3(lfT))
        c8 = gT[0:8, :] * LOG2E - bT[8:16, :]
        b_ref[c] = bT[8:16, :]
        crow_ref[c] = c8
        ccol_ref[c] = jnp.concatenate([c8, jnp.zeros((128 - 8, L), F32)], axis=0).T

    def chunk(c, ms):
        rows = pl.ds(pl.multiple_of(c * L, L), L)
        b8 = b_ref[c]
        c8 = crow_ref[c]
        c_cols = ccol_ref[c]
        inters = []
        for h in range(H):
            cols = slice(h * D, (h + 1) * D)
            q = q_ref[rows, cols]
            s_ref[h] = lax.dot_general(k_ref[rows, cols], q, _NT, preferred_element_type=F32)
            inters.append(lax.dot_general(c_ref[h].astype(BF16), q, _NT, preferred_element_type=F32))
        new_ms = []
        for h in range(H):
            cols = slice(h * D, (h + 1) * D)
            m_prev = ms[h]
            b_row = b8[h:h + 1, :]
            c_row = c8[h:h + 1, :]
            cm = jnp.where(causal, c_cols[:, h:h + 1], NEG)
            mx = jnp.maximum(m_prev, jnp.max(cm, axis=0, keepdims=True))
            w_inter = jnp.exp2(m_prev - mx)
            sT = s_ref[h] * jnp.exp2(cm - mx)
            vT = vT_ref[c, cols, :]
            inter = inters[h]
            num = w_inter * inter[0:D, :] + jnp.dot(vT, sT.astype(BF16), preferred_element_type=F32)
            den = w_inter * inter[D:D + 1, :] + jnp.sum(sT, axis=0, keepdims=True)
            m_t = b_row + mx
            hid = num * (1.0 / jnp.maximum(jnp.abs(den), jnp.exp2(-m_t)))
            hid = hid * lax.rsqrt(jnp.mean(hid * hid, axis=0, keepdims=True) + EPS) * gn_ref[cols, :]
            hT_ref[cols, :] = hid * _sigmoid(oT_ref[c, cols, :].astype(F32))

            m_last = m_t[:, L - 1:L]
            decay = w_inter[:, L - 1:L]
            w_last = jnp.exp2(c_row + (b_row[:, L - 1:L] - m_last))
            lhs = jnp.concatenate([vT.astype(F32) * w_last,
                                   jnp.where(first_row, w_last, 0.0)], axis=0).astype(BF16)
            c_ref[h] = decay * c_ref[h] + jnp.dot(lhs, k_ref[rows, cols], preferred_element_type=F32)
            new_ms.append(m_last)
        out_ref[rows, :] = hT_ref[...].T.astype(out_ref.dtype)
        return tuple(new_ms)

    lax.fori_loop(0, nc, chunk, tuple(jnp.zeros((1, 1), F32) for _ in range(H)))


def _ffn_kernel(x_ref, att_ref, mh_ref, wout_ref, g2_ref, wup_ref, cw_ref, cb_ref, wdown_ref, g3_ref,
                out_ref, halo_ref, *stage_refs):
    t = pl.program_id(1)
    tm = x_ref.shape[0]
    wa = att_ref.shape[1]
    dff = wdown_ref.shape[0]
    ch = FFN_COL_CHUNK
    nst = len(stage_refs)

    mix = (jnp.dot(att_ref[...], wout_ref[0:wa, :], preferred_element_type=F32)
           + jnp.dot(mh_ref[...], wout_ref[wa:, :], preferred_element_type=F32))
    h1 = x_ref[...] + mix
    a2 = _rms(h1, g2_ref[...]).astype(BF16)

    @pl.when(t == 0)
    def _():
        halo_ref[...] = jnp.zeros(halo_ref.shape, F32)

    nparts = FFN_ROW_PARTS
    tr = tm // nparts

    def project(c, r):
        st = stage_refs[c % nst]
        rows = slice(r * tr, (r + 1) * tr)
        for i, off in enumerate((c * ch, dff + c * ch)):
            st[i, HALO + r * tr:HALO + (r + 1) * tr, :] = jnp.dot(a2[rows, :], wup_ref[:, off:off + ch],
                                                                  preferred_element_type=F32)

    def gate(c, r):
        st = stage_refs[c % nst]
        ys = []
        for i, off in enumerate((c * ch, dff + c * ch)):
            cols = slice(off, off + ch)
            if r == 0:
                st[i, 0:HALO, :] = halo_ref[:, cols]
            y = cb_ref[:, cols]
            for j in range(FFN_CONV):
                o = HALO - (FFN_CONV - 1) + j + r * tr
                y = y + cw_ref[j:j + 1, cols] * st[i, o:o + tr, :]
            if r == nparts - 1:
                halo_ref[:, cols] = st[i, tm:tm + HALO, :]
            ys.append(y)
        return (_silu(ys[0]) * ys[1]).astype(BF16)

    def down(c, r, act):
        rows = slice(r * tr, (r + 1) * tr)
        out_ref[rows, :] += jnp.dot(act, wdown_ref[c * ch:(c + 1) * ch, :], preferred_element_type=F32)

    nch = dff // ch
    out_ref[...] = h1
    for c0 in range(min(2, nch)):
        for r in range(nparts):
            project(c0, r)
    for c in range(nch):
        for r in range(nparts):
            act = gate(c, r)
            down(c, r, act)
            if c + 2 < nch:
                project(c + 2, r)
    out_ref[...] = _rms(out_ref[...], g3_ref[...])


def _const_spec(shape):
    return pl.BlockSpec(shape, lambda *_: (0,) * len(shape))


def kernel(x, norm_mix_g, w_in, b_gates, mlstm_conv_w, mlstm_conv_b, att_out_g, mlstm_out_g, w_out, norm_ffn_g, w_up, ffn_conv_w, ffn_conv_b, w_down, norm_final_g):
    B, S, D = x.shape
    H, dh = ATT_HEADS, ATT_HEAD_DIM
    wa = H * dh
    vp = dh + MOBA_VPAD
    wm = MLSTM_HEADS * MLSTM_HEAD_DIM
    L = MOBA_BLOCK
    nb = S // L
    assert S % L == 0 and nb <= 8 and MLSTM_CHUNK == L
    assert w_in.shape[0] == 1 and w_in.shape[2] == 3 * wa + 4 * wm + 2 * MLSTM_HEADS
    dff = w_down.shape[1]
    assert dff % FFN_COL_CHUNK == 0

    wi = w_in[0]
    o_mq, o_mv, o_g = 3 * wa, 3 * wa + 2 * wm, 3 * wa + 4 * wm
    wn = jnp.concatenate([wi[:, wa:2 * wa], wi[:, o_mq:o_mv]], axis=1).astype(BF16)
    wt = jnp.concatenate([wi[:, 0:wa], wi[:, 2 * wa:3 * wa], wi[:, o_mv:o_g]], axis=1).T.astype(BF16)
    nh = MLSTM_HEADS
    wgT = jnp.zeros((16, D), F32).at[0:nh].set(wi[:, o_g:o_g + nh].T).at[8:8 + nh].set(wi[:, o_g + nh:].T).astype(BF16)
    bgT = jnp.zeros((16,), F32).at[0:nh].set(b_gates[0, 0:nh]).at[8:8 + nh].set(b_gates[0, nh:])
    tm = INPROJ_ROW_TILE
    assert S % tm == 0 and tm % L == 0
    bgT = jnp.broadcast_to(bgT.reshape(16, 1), (16, tm))
    g1 = norm_mix_g[0].reshape(1, D)
    cw1 = mlstm_conv_w[0]
    cb1 = mlstm_conv_b[0].reshape(1, 2 * wm)

    nbt = tm // L
    params = pltpu.CompilerParams(dimension_semantics=("arbitrary", "arbitrary"), vmem_limit_bytes=VMEM_LIMIT)
    tile = lambda w: pl.BlockSpec((None, tm, w), lambda b, t: (b, t, 0))
    btile = lambda r, c: pl.BlockSpec((None, nbt, r, c), lambda b, t: (b, t, 0, 0))
    bshape = lambda r, c, dt: jax.ShapeDtypeStruct((B, nb, r, c), dt)
    qT, kb, kmean, vTb, mq, mk, mvT, moT, gT = pl.pallas_call(
        _inproj_kernel,
        grid=(B, S // tm),
        in_specs=[tile(D), _const_spec((1, D)), _const_spec((D, wa + 2 * wm)), _const_spec((2 * wa + 2 * wm, D)),
                  _const_spec((16, D)), _const_spec((16, tm)),
                  _const_spec((MLSTM_CONV, 2 * wm)), _const_spec((1, 2 * wm))],
        out_specs=[pl.BlockSpec((None, wa, tm), lambda b, t: (b, 0, t)),
                   btile(L, wa), btile(1, wa), btile(H * vp, L),
                   tile(wm), tile(wm), btile(wm, L), btile(wm, L), btile(16, L)],
        out_shape=[jax.ShapeDtypeStruct((B, wa, S), BF16),
                   bshape(L, wa, BF16), bshape(1, wa, F32), bshape(H * vp, L, BF16),
                   jax.ShapeDtypeStruct((B, S, wm), BF16),
                   jax.ShapeDtypeStruct((B, S, wm), BF16),
                   bshape(wm, L, BF16), bshape(wm, L, BF16), bshape(16, L, F32)],
        scratch_shapes=[pltpu.VMEM((HALO + tm, 2 * wm), F32)],
        compiler_params=params,
    )(x, g1, wn, wt, wgT, bgT, cw1, cb1)

    km = kmean.reshape(B, nb, H, dh)
    km = jnp.pad(km, ((0, 0), (0, 8 - nb), (0, 0), (0, 0)))
    eye = jnp.eye(H, dtype=F32)
    kmt = jnp.einsum('bjhd,hg->bhjgd', km, eye).reshape(B, H * 8, wa)
    kmh = kmt.astype(BF16)
    kml = (kmt - kmh.astype(F32)).astype(BF16)
    ga = att_out_g[0].reshape(wa, 1)

    att = pl.pallas_call(
        _moba_kernel,
        grid=(B, nb),
        in_specs=[pl.BlockSpec((None, wa, L), lambda b, i: (b, 0, i)),
                  pl.BlockSpec((None, nb, L, wa), lambda b, i: (b, 0, 0, 0)),
                  pl.BlockSpec((None, nb, H * vp, L), lambda b, i: (b, 0, 0, 0)),
                  pl.BlockSpec((None, H * 8, wa), lambda b, i: (b, 0, 0)),
                  pl.BlockSpec((None, H * 8, wa), lambda b, i: (b, 0, 0)),
                  _const_spec((wa, 1))],
        out_specs=pl.BlockSpec((None, L, wa), lambda b, i: (b, i, 0)),
        out_shape=jax.ShapeDtypeStruct((B, S, wa), BF16),
        scratch_shapes=[pltpu.VMEM((H * 2 * dh, L), BF16), pltpu.VMEM((H * 8, L), F32), pltpu.VMEM((wa, L), F32),
                        pltpu.VMEM((H * vp, L), F32), pltpu.VMEM((H, L), F32), pltpu.VMEM((2, H, L), F32),
                        pltpu.VMEM((2, H, L), F32), pltpu.VMEM((2, H, L, L), F32)],
        compiler_params=params,
    )(qT, kb, vTb, kmh, kml, ga)

    seq = lambda w: pl.BlockSpec((None, S, w), lambda b: (b, 0, 0))
    blocks = lambda r: pl.BlockSpec((None, nb, r, L), lambda b: (b, 0, 0, 0))
    gn = jnp.broadcast_to(mlstm_out_g[0].reshape(wm, 1), (wm, L))
    mh = pl.pallas_call(
        _mlstm_kernel,
        grid=(B,),
        in_specs=[seq(wm), seq(wm), blocks(wm), blocks(wm), blocks(16), _const_spec((wm, L))],
        out_specs=seq(wm),
        out_shape=jax.ShapeDtypeStruct((B, S, wm), BF16),
        scratch_shapes=[pltpu.VMEM((MLSTM_HEADS, MLSTM_HEAD_DIM + 8, MLSTM_HEAD_DIM), F32),
                        pltpu.VMEM((MLSTM_HEADS, L, L), F32), pltpu.VMEM((wm, L), F32),
                        pltpu.VMEM((nb, 8, L), F32), pltpu.VMEM((nb, 8, L), F32), pltpu.VMEM((nb, L, 128), F32)],
        compiler_params=pltpu.CompilerParams(dimension_semantics=("arbitrary",), vmem_limit_bytes=VMEM_LIMIT),
    )(mq, mk, mvT, moT, gT, gn)

    tf = FFN_ROW_TILE
    assert S % tf == 0
    ftile = lambda w: pl.BlockSpec((None, tf, w), lambda b, t: (b, t, 0))
    out = pl.pallas_call(
        _ffn_kernel,
        grid=(B, S // tf),
        in_specs=[ftile(D), ftile(wa), ftile(wm), _const_spec((wa + wm, D)), _const_spec((1, D)),
                  _const_spec((D, 2 * dff)), _const_spec((FFN_CONV, 2 * dff)), _const_spec((1, 2 * dff)),
                  _const_spec((dff, D)), _const_spec((1, D))],
        out_specs=ftile(D),
        out_shape=jax.ShapeDtypeStruct((B, S, D), x.dtype),
        scratch_shapes=[pltpu.VMEM((HALO, 2 * dff), F32)]
        + [pltpu.VMEM((2, HALO + tf, FFN_COL_CHUNK), F32) for _ in range(FFN_STAGES)],
        compiler_params=params,
    )(x, att, mh, w_out[0].astype(BF16), norm_ffn_g[0].reshape(1, D), w_up[0].astype(BF16), ffn_conv_w[0],
      ffn_conv_b[0].reshape(1, 2 * dff), w_down[0].astype(BF16), norm_final_g.reshape(1, D))
    return out
```

```python
import functools

import jax
import jax.numpy as jnp
from jax import lax
from jax.experimental import pallas as pl
from jax.experimental.pallas import tpu as pltpu

EPS = 1e-6
ATT_HEADS = 8
ATT_HEAD_DIM = 64
MOBA_BLOCK = 256
MOBA_TOPK = 3
MLSTM_HEADS = 4
MLSTM_HEAD_DIM = 128
MLSTM_CONV = 4
FFN_CONV = 3
MLSTM_CHUNK = MOBA_BLOCK
INPROJ_ROW_TILE = 512
INPROJ_CONV_CHUNK = 256
HALO = 8
NEG = -1e30
MOBA_RUN = 4
MOBA_VPAD = 16
LOG2E = 1.4426950408889634
FFN_COL_CHUNK = 256
FFN_ROW_TILE = 512
FFN_STAGES = 3
VMEM_LIMIT = 56 * 1024 * 1024

F32 = jnp.float32
BF16 = jnp.bfloat16

_NT = (((1,), (1,)), ((), ()))
_TN = (((0,), (0,)), ((), ()))


def _sigmoid(y):
    return 1.0 / (1.0 + jnp.exp(-y))


def _rms(x, g):
    return x * lax.rsqrt(jnp.mean(x * x, axis=-1, keepdims=True) + EPS) * g


def _inproj_kernel(x_ref, g_ref, wn_ref, wt_ref, wgT_ref, bgT_ref, cw_ref, cb_ref,
                   qT_ref, k_ref, kmean_ref, vT_ref, mq_ref, mk_ref, mvT_ref, moT_ref, gT_ref,
                   conv_ref):
    t = pl.program_id(1)
    tm = x_ref.shape[0]
    L = MOBA_BLOCK
    wa = k_ref.shape[2]
    wm = mq_ref.shape[1]

    @pl.when(t == 0)
    def _():
        conv_ref[0:HALO, :] = jnp.zeros((HALO, 2 * wm), F32)

    a = _rms(x_ref[...], g_ref[...]).astype(BF16)
    cc = INPROJ_CONV_CHUNK

    def project_conv(j):
        cols = slice(j * cc, (j + 1) * cc)
        conv_ref[HALO:HALO + tm, cols] = jnp.dot(a, wn_ref[:, wa + j * cc:wa + (j + 1) * cc],
                                                 preferred_element_type=F32)

    def finish_conv(j):
        cols = slice(j * cc, (j + 1) * cc)
        y = cb_ref[:, cols]
        for tap in range(MLSTM_CONV):
            off = HALO - (MLSTM_CONV - 1) + tap
            y = y + cw_ref[tap:tap + 1, cols] * conv_ref[off:off + tm, cols]
        conv_ref[0:HALO, cols] = conv_ref[tm:tm + HALO, cols]
        y = y.astype(BF16)
        y = y * (1.0 / (1.0 + jnp.exp2(y * (-LOG2E))))
        if (j + 1) * cc <= wm:
            mq_ref[:, cols] = y * (MLSTM_HEAD_DIM ** -0.5)
        else:
            mk_ref[:, j * cc - wm:(j + 1) * cc - wm] = y

    def project_k():
        k = jnp.dot(a, wn_ref[:, 0:wa], preferred_element_type=F32)
        for i in range(tm // L):
            blk = k[i * L:(i + 1) * L, :]
            k_ref[i] = blk.astype(BF16)
            kmean_ref[i] = jnp.mean(blk, axis=0, keepdims=True)

    def transposed(lo, hi):
        return lax.dot_general(wt_ref[lo:hi, :], a, _NT, preferred_element_type=F32)

    def project_qT():
        qT_ref[...] = (transposed(0, wa) * (ATT_HEAD_DIM ** -0.5 * LOG2E)).astype(BF16)

    def project_blocks(ref, lo, hi):
        vals = transposed(lo, hi)
        for i in range(tm // L):
            ref[i] = vals[:, i * L:(i + 1) * L].astype(BF16)

    def project_vT():
        vals = transposed(wa, 2 * wa)
        dh = ATT_HEAD_DIM
        vp = dh + MOBA_VPAD
        ones_row = (lax.broadcasted_iota(jnp.int32, (MOBA_VPAD, L), 0) == 0).astype(BF16)
        for i in range(tm // L):
            for h in range(ATT_HEADS):
                vT_ref[i, h * vp:h * vp + dh, :] = vals[h * dh:(h + 1) * dh, i * L:(i + 1) * L].astype(BF16)
                vT_ref[i, h * vp + dh:(h + 1) * vp, :] = ones_row

    def project_gates():
        gT = lax.dot_general(wgT_ref[...], a, _NT, preferred_element_type=F32) + bgT_ref[...]
        for i in range(tm // L):
            gT_ref[i] = gT[:, i * L:(i + 1) * L]

    others = [project_k, project_qT, project_vT,
              lambda: project_blocks(mvT_ref, 2 * wa, 2 * wa + wm),
              lambda: project_blocks(moT_ref, 2 * wa + wm, 2 * wa + 2 * wm), project_gates]
    nconv = 2 * wm // cc
    project_conv(0)
    for j in range(nconv):
        if j + 1 < nconv:
            project_conv(j + 1)
        if others:
            others.pop(0)()
        finish_conv(j)
    for f in others:
        f()


def _moba_kernel(qT_ref, k_ref, vT_ref, kmh_ref, kml_ref, g_ref, o_ref, qz_ref, bias_ref, oT_ref, acc_ref, m_ref,
                 alpha_ref, sub_ref, s_ref):
    i = pl.program_id(1)
    L = MOBA_BLOCK
    dh = ATT_HEAD_DIM
    nbp = 8
    def prepare():
        qT = qT_ref[...]
        zeros = jnp.zeros((dh, L), BF16)
        for h in range(ATT_HEADS):
            qh = qT[h * dh:(h + 1) * dh, :]
            lo, hi = (qh, zeros) if h % 2 == 0 else (zeros, qh)
            qz_ref[h * 2 * dh:h * 2 * dh + dh, :] = lo
            qz_ref[h * 2 * dh + dh:(h + 1) * 2 * dh, :] = hi
        gate = (jnp.dot(kmh_ref[...], qT, preferred_element_type=F32)
                + jnp.dot(kml_ref[...], qT, preferred_element_type=F32))
        jrow = lax.broadcasted_iota(jnp.int32, (nbp, L), 0)
        past = jrow < i
        for h in range(ATT_HEADS):
            g = gate[h * nbp:(h + 1) * nbp, :]
            gm = jnp.where(past, g, -jnp.inf)
            rank = jnp.zeros((nbp, L), jnp.int32)
            for ii in range(nbp):
                gi = gm[ii:ii + 1, :]
                beats = jnp.where(jrow > ii, jnp.where(gi >= g, 1, 0), jnp.where(gi > g, 1, 0))
                rank = rank + beats
            sel = past & (rank < MOBA_TOPK)
            bias_ref[h * nbp:(h + 1) * nbp, :] = jnp.where(sel, 0.0, NEG)

    krow = lax.broadcasted_iota(jnp.int32, (L, L), 0)
    qcol = lax.broadcasted_iota(jnp.int32, (L, L), 1)
    causal = krow <= qcol

    vp = dh + MOBA_VPAD

    def scores(j, slot, own):
        for p in range(ATT_HEADS // 2):
            k_j = k_ref[j, :, p * 2 * dh:(p + 1) * 2 * dh]
            for h in (2 * p, 2 * p + 1):
                s = jnp.dot(k_j, qz_ref[h * 2 * dh:(h + 1) * 2 * dh, :], preferred_element_type=F32)
                if own:
                    s = jnp.where(causal, s, NEG)
                    m_new = jnp.max(s, axis=0, keepdims=True)
                    alpha = jnp.zeros((1, L), F32)
                    sub = m_new
                else:
                    b = bias_ref[pl.ds(h * nbp + j, 1), :]
                    m = m_ref[h:h + 1, :]
                    m_new = jnp.maximum(m, jnp.max(s, axis=0, keepdims=True) + b)
                    alpha = jnp.exp2(m - m_new)
                    sub = m_new - b
                s_ref[slot, h] = s
                m_ref[h:h + 1, :] = m_new
                alpha_ref[slot, pl.ds(h, 1), :] = alpha
                sub_ref[slot, pl.ds(h, 1), :] = sub

    def values(j, slot):
        for h in range(ATT_HEADS):
            rows = slice(h * vp, (h + 1) * vp)
            pr = jnp.exp2(s_ref[slot, h] - sub_ref[slot, pl.ds(h, 1), :])
            pv = jnp.dot(vT_ref[j, rows, :], pr.astype(BF16), preferred_element_type=F32)
            acc_ref[rows, :] = alpha_ref[slot, pl.ds(h, 1), :] * acc_ref[rows, :] + pv

    def visit_run(blocks):
        scores(blocks[0][0], 0, blocks[0][1])
        for r, (j, _) in enumerate(blocks):
            if r + 1 < len(blocks):
                scores(blocks[r + 1][0], (r + 1) % 2, blocks[r + 1][1])
            values(j, r % 2)

    acc_ref[...] = jnp.zeros(acc_ref.shape, F32)
    tail = lax.rem(i, MOBA_RUN)
    for n in range(MOBA_RUN):
        @pl.when(tail == n)
        def _(n=n):
            prepare()
            visit_run([(i, True)] + [(r, False) for r in range(n)])

    def full_run(t, carry):
        visit_run([(tail + MOBA_RUN * t + r, False) for r in range(MOBA_RUN)])
        return carry

    lax.fori_loop(0, i // MOBA_RUN, full_run, 0)

    for h in range(ATT_HEADS):
        rows = slice(h * dh, (h + 1) * dh)
        acc = acc_ref[h * vp:(h + 1) * vp, :]
        o = acc[0:dh, :] * (1.0 / acc[dh:dh + 1, :])
        oT_ref[rows, :] = o * lax.rsqrt(jnp.mean(o * o, axis=0, keepdims=True) + EPS) * g_ref[rows, :]

    o_ref[...] = oT_ref[...].T.astype(BF16)


def _split3(x):
    hi = x.astype(BF16)
    r = x - hi.astype(F32)
    mid = r.astype(BF16)
    lo = (r - mid.astype(F32)).astype(BF16)
    return hi, mid, lo


def _mlstm_kernel(q_ref, k_ref, vT_ref, oT_ref, gT_ref, gn_ref, out_ref, c_ref, s_ref, hT_ref, b_ref, crow_ref,
                  ccol_ref):
    nc, _, L = vT_ref.shape
    D = MLSTM_HEAD_DIM
    H = MLSTM_HEADS
    c_ref[...] = jnp.zeros(c_ref.shape, F32)
    srow = lax.broadcasted_iota(jnp.int32, (L, L), 0)
    tcol = lax.broadcasted_iota(jnp.int32, (L, L), 1)
    causal = srow <= tcol
    upper = causal.astype(BF16)
    first_row = lax.broadcasted_iota(jnp.int32, (8, L), 0) == 0

    for c in range(nc):
        gT = gT_ref[c]
        lfT = (jnp.minimum(gT, 0.0) - jnp.log(1.0 + jnp.exp(-jnp.abs(gT)))) * LOG2E
        bT = sum(jnp.dot(part, upper, preferred_element_type=F32) for part in _split3(lfT))
        c8 = gT[0:8, :] * LOG2E - bT[8:16, :]
        b_ref[c] = bT[8:16, :]
        crow_ref[c] = c8
        ccol_ref[c] = jnp.concatenate([c8, jnp.zeros((128 - 8, L), F32)], axis=0).T

    def chunk(c, ms):
        rows = pl.ds(pl.multiple_of(c * L, L), L)
        b8 = b_ref[c]
        c8 = crow_ref[c]
        c_cols = ccol_ref[c]
        inters = []
        for h in range(H):
            cols = slice(h * D, (h + 1) * D)
            q = q_ref[rows, cols]
            s_ref[h] = lax.dot_general(k_ref[rows, cols], q, _NT, preferred_element_type=F32)
            inters.append(lax.dot_general(c_ref[h].astype(BF16), q, _NT, preferred_element_type=F32))
        new_ms = []
        for h in range(H):
            cols = slice(h * D, (h + 1) * D)
            m_prev = ms[h]
            b_row = b8[h:h + 1, :]
            c_row = c8[h:h + 1, :]
            cm = jnp.where(causal, c_cols[:, h:h + 1], NEG)
            mx = jnp.maximum(m_prev, jnp.max(cm, axis=0, keepdims=True))
            w_inter = jnp.exp2(m_prev - mx)
            sT = s_ref[h] * jnp.exp2(cm - mx)
            vT = vT_ref[c, cols, :]
            inter = inters[h]
            num = w_inter * inter[0:D, :] + jnp.dot(vT, sT.astype(BF16), preferred_element_type=F32)
            den = w_inter * inter[D:D + 1, :] + jnp.sum(sT, axis=0, keepdims=True)
            m_t = b_row + mx
            hid = num * (1.0 / jnp.maximum(jnp.abs(den), jnp.exp2(-m_t)))
            hid = hid * lax.rsqrt(jnp.mean(hid * hid, axis=0, keepdims=True) + EPS) * gn_ref[cols, :]
            hT_ref[cols, :] = hid * _sigmoid(oT_ref[c, cols, :].astype(F32))

            m_last = m_t[:, L - 1:L]
            decay = w_inter[:, L - 1:L]
            w_last = jnp.exp2(c_row + (b_row[:, L - 1:L] - m_last))
            lhs = jnp.concatenate([vT.astype(F32) * w_last,
                                   jnp.where(first_row, w_last, 0.0)], axis=0).astype(BF16)
            c_ref[h] = decay * c_ref[h] + jnp.dot(lhs, k_ref[rows, cols], preferred_element_type=F32)
            new_ms.append(m_last)
        out_ref[rows, :] = hT_ref[...].T.astype(out_ref.dtype)
        return tuple(new_ms)

    lax.fori_loop(0, nc, chunk, tuple(jnp.zeros((1, 1), F32) for _ in range(H)))


def _ffn_kernel(x_ref, att_ref, mh_ref, wout_ref, g2_ref, wup_ref, cw_ref, cb_ref, wdown_ref, g3_ref,
                out_ref, halo_ref, *stage_refs):
    t = pl.program_id(1)
    tm = x_ref.shape[0]
    wa = att_ref.shape[1]
    dff = wdown_ref.shape[0]
    ch = FFN_COL_CHUNK
    nst = len(stage_refs)

    mix = (jnp.dot(att_ref[...], wout_ref[0:wa, :], preferred_element_type=F32)
           + jnp.dot(mh_ref[...], wout_ref[wa:, :], preferred_element_type=F32))
    h1 = x_ref[...] + mix
    a2 = _rms(h1, g2_ref[...]).astype(BF16)

    @pl.when(t == 0)
    def _():
        halo_ref[...] = jnp.zeros(halo_ref.shape, F32)

    def project(c):
        st = stage_refs[c % nst]
        for i, off in enumerate((c * ch, dff + c * ch)):
            st[i, HALO:HALO + tm, :] = jnp.dot(a2, wup_ref[:, off:off + ch], preferred_element_type=F32)

    def gate(c):
        st = stage_refs[c % nst]
        ys = []
        for i, off in enumerate((c * ch, dff + c * ch)):
            cols = slice(off, off + ch)
            st[i, 0:HALO, :] = halo_ref[:, cols]
            y = cb_ref[:, cols]
            for j in range(FFN_CONV):
                o = HALO - (FFN_CONV - 1) + j
                y = y + cw_ref[j:j + 1, cols] * st[i, o:o + tm, :]
            halo_ref[:, cols] = st[i, tm:tm + HALO, :]
            ys.append(y.astype(BF16))
        sig = 1.0 / (1.0 + jnp.exp2(ys[0] * (-LOG2E)))
        return ys[0] * sig * ys[1]

    nch = dff // ch
    out_ref[...] = h1
    project(0)
    project(1)
    act = gate(0)
    for c in range(nch):
        if c + 2 < nch:
            project(c + 2)
        act_next = gate(c + 1) if c + 1 < nch else None
        out_ref[...] += jnp.dot(act, wdown_ref[c * ch:(c + 1) * ch, :], preferred_element_type=F32)
        act = act_next
    out_ref[...] = _rms(out_ref[...], g3_ref[...])


def _const_spec(shape):
    return pl.BlockSpec(shape, lambda *_: (0,) * len(shape))


def kernel(x, norm_mix_g, w_in, b_gates, mlstm_conv_w, mlstm_conv_b, att_out_g, mlstm_out_g, w_out, norm_ffn_g, w_up, ffn_conv_w, ffn_conv_b, w_down, norm_final_g):
    B, S, D = x.shape
    H, dh = ATT_HEADS, ATT_HEAD_DIM
    wa = H * dh
    vp = dh + MOBA_VPAD
    wm = MLSTM_HEADS * MLSTM_HEAD_DIM
    L = MOBA_BLOCK
    nb = S // L
    assert S % L == 0 and nb <= 8 and MLSTM_CHUNK == L
    assert w_in.shape[0] == 1 and w_in.shape[2] == 3 * wa + 4 * wm + 2 * MLSTM_HEADS
    dff = w_down.shape[1]
    assert dff % FFN_COL_CHUNK == 0

    wi = w_in[0]
    o_mq, o_mv, o_g = 3 * wa, 3 * wa + 2 * wm, 3 * wa + 4 * wm
    wn = jnp.concatenate([wi[:, wa:2 * wa], wi[:, o_mq:o_mv]], axis=1).astype(BF16)
    wt = jnp.concatenate([wi[:, 0:wa], wi[:, 2 * wa:3 * wa], wi[:, o_mv:o_g]], axis=1).T.astype(BF16)
    nh = MLSTM_HEADS
    wgT = jnp.zeros((16, D), F32).at[0:nh].set(wi[:, o_g:o_g + nh].T).at[8:8 + nh].set(wi[:, o_g + nh:].T).astype(BF16)
    bgT = jnp.zeros((16,), F32).at[0:nh].set(b_gates[0, 0:nh]).at[8:8 + nh].set(b_gates[0, nh:])
    tm = INPROJ_ROW_TILE
    assert S % tm == 0 and tm % L == 0
    bgT = jnp.broadcast_to(bgT.reshape(16, 1), (16, tm))
    g1 = norm_mix_g[0].reshape(1, D)
    cw1 = mlstm_conv_w[0]
    cb1 = mlstm_conv_b[0].reshape(1, 2 * wm)

    nbt = tm // L
    params = pltpu.CompilerParams(dimension_semantics=("arbitrary", "arbitrary"), vmem_limit_bytes=VMEM_LIMIT)
    tile = lambda w: pl.BlockSpec((None, tm, w), lambda b, t: (b, t, 0))
    btile = lambda r, c: pl.BlockSpec((None, nbt, r, c), lambda b, t: (b, t, 0, 0))
    bshape = lambda r, c, dt: jax.ShapeDtypeStruct((B, nb, r, c), dt)
    qT, kb, kmean, vTb, mq, mk, mvT, moT, gT = pl.pallas_call(
        _inproj_kernel,
        grid=(B, S // tm),
        in_specs=[tile(D), _const_spec((1, D)), _const_spec((D, wa + 2 * wm)), _const_spec((2 * wa + 2 * wm, D)),
                  _const_spec((16, D)), _const_spec((16, tm)),
                  _const_spec((MLSTM_CONV, 2 * wm)), _const_spec((1, 2 * wm))],
        out_specs=[pl.BlockSpec((None, wa, tm), lambda b, t: (b, 0, t)),
                   btile(L, wa), btile(1, wa), btile(H * vp, L),
                   tile(wm), tile(wm), btile(wm, L), btile(wm, L), btile(16, L)],
        out_shape=[jax.ShapeDtypeStruct((B, wa, S), BF16),
                   bshape(L, wa, BF16), bshape(1, wa, F32), bshape(H * vp, L, BF16),
                   jax.ShapeDtypeStruct((B, S, wm), BF16),
                   jax.ShapeDtypeStruct((B, S, wm), BF16),
                   bshape(wm, L, BF16), bshape(wm, L, BF16), bshape(16, L, F32)],
        scratch_shapes=[pltpu.VMEM((HALO + tm, 2 * wm), F32)],
        compiler_params=params,
    )(x, g1, wn, wt, wgT, bgT, cw1, cb1)

    km = kmean.reshape(B, nb, H, dh)
    km = jnp.pad(km, ((0, 0), (0, 8 - nb), (0, 0), (0, 0)))
    eye = jnp.eye(H, dtype=F32)
    kmt = jnp.einsum('bjhd,hg->bhjgd', km, eye).reshape(B, H * 8, wa)
    kmh = kmt.astype(BF16)
    kml = (kmt - kmh.astype(F32)).astype(BF16)
    ga = att_out_g[0].reshape(wa, 1)

    att = pl.pallas_call(
        _moba_kernel,
        grid=(B, nb),
        in_specs=[pl.BlockSpec((None, wa, L), lambda b, i: (b, 0, i)),
                  pl.BlockSpec((None, nb, L, wa), lambda b, i: (b, 0, 0, 0)),
                  pl.BlockSpec((None, nb, H * vp, L), lambda b, i: (b, 0, 0, 0)),
                  pl.BlockSpec((None, H * 8, wa), lambda b, i: (b, 0, 0)),
                  pl.BlockSpec((None, H * 8, wa), lambda b, i: (b, 0, 0)),
                  _const_spec((wa, 1))],
        out_specs=pl.BlockSpec((None, L, wa), lambda b, i: (b, i, 0)),
        out_shape=jax.ShapeDtypeStruct((B, S, wa), BF16),
        scratch_shapes=[pltpu.VMEM((H * 2 * dh, L), BF16), pltpu.VMEM((H * 8, L), F32), pltpu.VMEM((wa, L), F32),
                        pltpu.VMEM((H * vp, L), F32), pltpu.VMEM((H, L), F32), pltpu.VMEM((2, H, L), F32),
                        pltpu.VMEM((2, H, L), F32), pltpu.VMEM((2, H, L, L), F32)],
        compiler_params=params,
    )(qT, kb, vTb, kmh, kml, ga)

    seq = lambda w: pl.BlockSpec((None, S, w), lambda b: (b, 0, 0))
    blocks = lambda r: pl.BlockSpec((None, nb, r, L), lambda b: (b, 0, 0, 0))
    gn = jnp.broadcast_to(mlstm_out_g[0].reshape(wm, 1), (wm, L))
    mh = pl.pallas_call(
        _mlstm_kernel,
        grid=(B,),
        in_specs=[seq(wm), seq(wm), blocks(wm), blocks(wm), blocks(16), _const_spec((wm, L))],
        out_specs=seq(wm),
        out_shape=jax.ShapeDtypeStruct((B, S, wm), BF16),
        scratch_shapes=[pltpu.VMEM((MLSTM_HEADS, MLSTM_HEAD_DIM + 8, MLSTM_HEAD_DIM), F32),
                        pltpu.VMEM((MLSTM_HEADS, L, L), F32), pltpu.VMEM((wm, L), F32),
                        pltpu.VMEM((nb, 8, L), F32), pltpu.VMEM((nb, 8, L), F32), pltpu.VMEM((nb, L, 128), F32)],
        compiler_params=pltpu.CompilerParams(dimension_semantics=("arbitrary",), vmem_limit_bytes=VMEM_LIMIT),
    )(mq, mk, mvT, moT, gT, gn)

    tf = FFN_ROW_TILE
    assert S % tf == 0
    ftile = lambda w: pl.BlockSpec((None, tf, w), lambda b, t: (b, t, 0))
    out = pl.pallas_call(
        _ffn_kernel,
        grid=(B, S // tf),
        in_specs=[ftile(D), ftile(wa), ftile(wm), _const_spec((wa + wm, D)), _const_spec((1, D)),
                  _const_spec((D, 2 * dff)), _const_spec((FFN_CONV, 2 * dff)), _const_spec((1, 2 * dff)),
                  _const_spec((dff, D)), _const_spec((1, D))],
        out_specs=ftile(D),
        out_shape=jax.ShapeDtypeStruct((B, S, D), x.dtype),
        scratch_shapes=[pltpu.VMEM((HALO, 2 * dff), F32)]
        + [pltpu.VMEM((2, HALO + tf, FFN_COL_CHUNK), F32) for _ in range(FFN_STAGES)],
        compiler_params=params,
    )(x, att, mh, w_out[0].astype(BF16), norm_ffn_g[0].reshape(1, D), w_up[0].astype(BF16), ffn_conv_w[0],
      ffn_conv_b[0].reshape(1, 2 * dff), w_down[0].astype(BF16), norm_final_g.reshape(1, D))
    return out
```

```python
import functools

import jax
import jax.numpy as jnp
from jax import lax
from jax.experimental import pallas as pl
from jax.experimental.pallas import tpu as pltpu

EPS = 1e-6
ATT_HEADS = 8
ATT_HEAD_DIM = 64
MOBA_BLOCK = 256
MOBA_TOPK = 3
MLSTM_HEADS = 4
MLSTM_HEAD_DIM = 128
MLSTM_CONV = 4
FFN_CONV = 3
MLSTM_CHUNK = MOBA_BLOCK
INPROJ_ROW_TILE = 512
INPROJ_CONV_CHUNK = 256
HALO = 8
NEG = -1e30
MOBA_RUN = 4
MOBA_VPAD = 16
LOG2E = 1.4426950408889634
FFN_COL_CHUNK = 256
FFN_ROW_TILE = 1024
FFN_STAGES = 3
VMEM_LIMIT = 56 * 1024 * 1024

F32 = jnp.float32
BF16 = jnp.bfloat16

_NT = (((1,), (1,)), ((), ()))
_TN = (((0,), (0,)), ((), ()))


def _sigmoid(y):
    return 1.0 / (1.0 + jnp.exp(-y))


def _rms(x, g):
    return x * lax.rsqrt(jnp.mean(x * x, axis=-1, keepdims=True) + EPS) * g


def _inproj_kernel(x_ref, g_ref, wn_ref, wt_ref, wgT_ref, bgT_ref, cw_ref, cb_ref,
                   qT_ref, k_ref, kmean_ref, vT_ref, mq_ref, mk_ref, mvT_ref, moT_ref, gT_ref,
                   conv_ref):
    t = pl.program_id(1)
    tm = x_ref.shape[0]
    L = MOBA_BLOCK
    wa = k_ref.shape[2]
    wm = mq_ref.shape[1]

    @pl.when(t == 0)
    def _():
        conv_ref[0:HALO, :] = jnp.zeros((HALO, 2 * wm), F32)

    a = _rms(x_ref[...], g_ref[...]).astype(BF16)
    cc = INPROJ_CONV_CHUNK

    def project_conv(j):
        cols = slice(j * cc, (j + 1) * cc)
        conv_ref[HALO:HALO + tm, cols] = jnp.dot(a, wn_ref[:, wa + j * cc:wa + (j + 1) * cc],
                                                 preferred_element_type=F32)

    def finish_conv(j):
        cols = slice(j * cc, (j + 1) * cc)
        y = cb_ref[:, cols]
        for tap in range(MLSTM_CONV):
            off = HALO - (MLSTM_CONV - 1) + tap
            y = y + cw_ref[tap:tap + 1, cols] * conv_ref[off:off + tm, cols]
        conv_ref[0:HALO, cols] = conv_ref[tm:tm + HALO, cols]
        y = y.astype(BF16)
        y = y * (1.0 / (1.0 + jnp.exp2(y * (-LOG2E))))
        if (j + 1) * cc <= wm:
            mq_ref[:, cols] = y * (MLSTM_HEAD_DIM ** -0.5)
        else:
            mk_ref[:, j * cc - wm:(j + 1) * cc - wm] = y

    def project_k():
        k = jnp.dot(a, wn_ref[:, 0:wa], preferred_element_type=F32)
        for i in range(tm // L):
            blk = k[i * L:(i + 1) * L, :]
            k_ref[i] = blk.astype(BF16)
            kmean_ref[i] = jnp.mean(blk, axis=0, keepdims=True)

    def transposed(lo, hi):
        return lax.dot_general(wt_ref[lo:hi, :], a, _NT, preferred_element_type=F32)

    def project_qT():
        qT_ref[...] = (transposed(0, wa) * (ATT_HEAD_DIM ** -0.5 * LOG2E)).astype(BF16)

    def project_blocks(ref, lo, hi):
        vals = transposed(lo, hi)
        for i in range(tm // L):
            ref[i] = vals[:, i * L:(i + 1) * L].astype(BF16)

    def project_vT():
        vals = transposed(wa, 2 * wa)
        dh = ATT_HEAD_DIM
        vp = dh + MOBA_VPAD
        ones_row = (lax.broadcasted_iota(jnp.int32, (MOBA_VPAD, L), 0) == 0).astype(BF16)
        for i in range(tm // L):
            for h in range(ATT_HEADS):
                vT_ref[i, h * vp:h * vp + dh, :] = vals[h * dh:(h + 1) * dh, i * L:(i + 1) * L].astype(BF16)
                vT_ref[i, h * vp + dh:(h + 1) * vp, :] = ones_row

    def project_gates():
        gT = lax.dot_general(wgT_ref[...], a, _NT, preferred_element_type=F32) + bgT_ref[...]
        for i in range(tm // L):
            gT_ref[i] = gT[:, i * L:(i + 1) * L]

    others = [project_k, project_qT, project_vT,
              lambda: project_blocks(mvT_ref, 2 * wa, 2 * wa + wm),
              lambda: project_blocks(moT_ref, 2 * wa + wm, 2 * wa + 2 * wm), project_gates]
    nconv = 2 * wm // cc
    project_conv(0)
    for j in range(nconv):
        if j + 1 < nconv:
            project_conv(j + 1)
        if others:
            others.pop(0)()
        finish_conv(j)
    for f in others:
        f()


def _moba_kernel(qT_ref, k_ref, vT_ref, kmh_ref, kml_ref, g_ref, o_ref, qz_ref, bias_ref, oT_ref, acc_ref, m_ref,
                 alpha_ref, sub_ref, s_ref):
    i = pl.program_id(1)
    L = MOBA_BLOCK
    dh = ATT_HEAD_DIM
    nbp = 8
    def prepare():
        qT = qT_ref[...]
        zeros = jnp.zeros((dh, L), BF16)
        for h in range(ATT_HEADS):
            qh = qT[h * dh:(h + 1) * dh, :]
            lo, hi = (qh, zeros) if h % 2 == 0 else (zeros, qh)
            qz_ref[h * 2 * dh:h * 2 * dh + dh, :] = lo
            qz_ref[h * 2 * dh + dh:(h + 1) * 2 * dh, :] = hi
        gate = (jnp.dot(kmh_ref[...], qT, preferred_element_type=F32)
                + jnp.dot(kml_ref[...], qT, preferred_element_type=F32))
        jrow = lax.broadcasted_iota(jnp.int32, (nbp, L), 0)
        past = jrow < i
        for h in range(ATT_HEADS):
            g = gate[h * nbp:(h + 1) * nbp, :]
            gm = jnp.where(past, g, -jnp.inf)
            rank = jnp.zeros((nbp, L), jnp.int32)
            for ii in range(nbp):
                gi = gm[ii:ii + 1, :]
                beats = jnp.where(jrow > ii, jnp.where(gi >= g, 1, 0), jnp.where(gi > g, 1, 0))
                rank = rank + beats
            sel = past & (rank < MOBA_TOPK)
            bias_ref[h * nbp:(h + 1) * nbp, :] = jnp.where(sel, 0.0, NEG)

    krow = lax.broadcasted_iota(jnp.int32, (L, L), 0)
    qcol = lax.broadcasted_iota(jnp.int32, (L, L), 1)
    causal = krow <= qcol

    vp = dh + MOBA_VPAD

    def scores(j, slot, own):
        for p in range(ATT_HEADS // 2):
            k_j = k_ref[j, :, p * 2 * dh:(p + 1) * 2 * dh]
            for h in (2 * p, 2 * p + 1):
                s = jnp.dot(k_j, qz_ref[h * 2 * dh:(h + 1) * 2 * dh, :], preferred_element_type=F32)
                if own:
                    s = jnp.where(causal, s, NEG)
                    m_new = jnp.max(s, axis=0, keepdims=True)
                    alpha = jnp.zeros((1, L), F32)
                    sub = m_new
                else:
                    b = bias_ref[pl.ds(h * nbp + j, 1), :]
                    m = m_ref[h:h + 1, :]
                    m_new = jnp.maximum(m, jnp.max(s, axis=0, keepdims=True) + b)
                    alpha = jnp.exp2(m - m_new)
                    sub = m_new - b
                s_ref[slot, h] = s
                m_ref[h:h + 1, :] = m_new
                alpha_ref[slot, pl.ds(h, 1), :] = alpha
                sub_ref[slot, pl.ds(h, 1), :] = sub

    def values(j, slot):
        for h in range(ATT_HEADS):
            rows = slice(h * vp, (h + 1) * vp)
            pr = jnp.exp2(s_ref[slot, h] - sub_ref[slot, pl.ds(h, 1), :])
            pv = jnp.dot(vT_ref[j, rows, :], pr.astype(BF16), preferred_element_type=F32)
            acc_ref[rows, :] = alpha_ref[slot, pl.ds(h, 1), :] * acc_ref[rows, :] + pv

    def visit_run(blocks):
        scores(blocks[0][0], 0, blocks[0][1])
        for r, (j, _) in enumerate(blocks):
            if r + 1 < len(blocks):
                scores(blocks[r + 1][0], (r + 1) % 2, blocks[r + 1][1])
            values(j, r % 2)

    acc_ref[...] = jnp.zeros(acc_ref.shape, F32)
    tail = lax.rem(i, MOBA_RUN)
    for n in range(MOBA_RUN):
        @pl.when(tail == n)
        def _(n=n):
            prepare()
            visit_run([(i, True)] + [(r, False) for r in range(n)])

    def full_run(t, carry):
        visit_run([(tail + MOBA_RUN * t + r, False) for r in range(MOBA_RUN)])
        return carry

    lax.fori_loop(0, i // MOBA_RUN, full_run, 0)

    for h in range(ATT_HEADS):
        rows = slice(h * dh, (h + 1) * dh)
        acc = acc_ref[h * vp:(h + 1) * vp, :]
        o = acc[0:dh, :] * (1.0 / acc[dh:dh + 1, :])
        oT_ref[rows, :] = o * lax.rsqrt(jnp.mean(o * o, axis=0, keepdims=True) + EPS) * g_ref[rows, :]

    o_ref[...] = oT_ref[...].T.astype(BF16)


def _split3(x):
    hi = x.astype(BF16)
    r = x - hi.astype(F32)
    mid = r.astype(BF16)
    lo = (r - mid.astype(F32)).astype(BF16)
    return hi, mid, lo


def _mlstm_kernel(q_ref, k_ref, vT_ref, oT_ref, gT_ref, gn_ref, out_ref, c_ref, s_ref, hT_ref, b_ref, crow_ref,
                  ccol_ref):
    nc, _, L = vT_ref.shape
    D = MLSTM_HEAD_DIM
    H = MLSTM_HEADS
    c_ref[...] = jnp.zeros(c_ref.shape, F32)
    srow = lax.broadcasted_iota(jnp.int32, (L, L), 0)
    tcol = lax.broadcasted_iota(jnp.int32, (L, L), 1)
    causal = srow <= tcol
    upper = causal.astype(BF16)
    first_row = lax.broadcasted_iota(jnp.int32, (8, L), 0) == 0

    for c in range(nc):
        gT = gT_ref[c]
        lfT = (jnp.minimum(gT, 0.0) - jnp.log(1.0 + jnp.exp(-jnp.abs(gT)))) * LOG2E
        bT = sum(jnp.dot(part, upper, preferred_element_type=F32) for part in _split3(lfT))
        c8 = gT[0:8, :] * LOG2E - bT[8:16, :]
        b_ref[c] = bT[8:16, :]
        crow_ref[c] = c8
        ccol_ref[c] = jnp.concatenate([c8, jnp.zeros((128 - 8, L), F32)], axis=0).T

    def chunk(c, ms):
        rows = pl.ds(pl.multiple_of(c * L, L), L)
        b8 = b_ref[c]
        c8 = crow_ref[c]
        c_cols = ccol_ref[c]
        inters = []
        for h in range(H):
            cols = slice(h * D, (h + 1) * D)
            q = q_ref[rows, cols]
            s_ref[h] = lax.dot_general(k_ref[rows, cols], q, _NT, preferred_element_type=F32)
            inters.append(lax.dot_general(c_ref[h].astype(BF16), q, _NT, preferred_element_type=F32))
        new_ms = []
        for h in range(H):
            cols = slice(h * D, (h + 1) * D)
            m_prev = ms[h]
            b_row = b8[h:h + 1, :]
            c_row = c8[h:h + 1, :]
            cm = jnp.where(causal, c_cols[:, h:h + 1], NEG)
            mx = jnp.maximum(m_prev, jnp.max(cm, axis=0, keepdims=True))
            w_inter = jnp.exp2(m_prev - mx)
            sT = s_ref[h] * jnp.exp2(cm - mx)
            vT = vT_ref[c, cols, :]
            inter = inters[h]
            num = w_inter * inter[0:D, :] + jnp.dot(vT, sT.astype(BF16), preferred_element_type=F32)
            den = w_inter * inter[D:D + 1, :] + jnp.sum(sT, axis=0, keepdims=True)
            m_t = b_row + mx
            hid = num * (1.0 / jnp.maximum(jnp.abs(den), jnp.exp2(-m_t)))
            hid = hid * lax.rsqrt(jnp.mean(hid * hid, axis=0, keepdims=True) + EPS) * gn_ref[cols, :]
            hT_ref[cols, :] = hid * _sigmoid(oT_ref[c, cols, :].astype(F32))

            m_last = m_t[:, L - 1:L]
            decay = w_inter[:, L - 1:L]
            w_last = jnp.exp2(c_row + (b_row[:, L - 1:L] - m_last))
            lhs = jnp.concatenate([vT.astype(F32) * w_last,
                                   jnp.where(first_row, w_last, 0.0)], axis=0).astype(BF16)
            c_ref[h] = decay * c_ref[h] + jnp.dot(lhs, k_ref[rows, cols], preferred_element_type=F32)
            new_ms.append(m_last)
        out_ref[rows, :] = hT_ref[...].T.astype(out_ref.dtype)
        return tuple(new_ms)

    lax.fori_loop(0, nc, chunk, tuple(jnp.zeros((1, 1), F32) for _ in range(H)))


def _ffn_kernel(x_ref, att_ref, mh_ref, wout_ref, g2_ref, wup_ref, cw_ref, cb_ref, wdown_ref, g3_ref,
                out_ref, halo_ref, *stage_refs):
    t = pl.program_id(1)
    tm = x_ref.shape[0]
    dff = wdown_ref.shape[0]
    ch = FFN_COL_CHUNK
    nst = len(stage_refs)

    heads = jnp.concatenate([att_ref[...], mh_ref[...]], axis=1)
    h1 = x_ref[...] + jnp.dot(heads, wout_ref[...], preferred_element_type=F32)
    a2 = _rms(h1, g2_ref[...]).astype(BF16)

    @pl.when(t == 0)
    def _():
        halo_ref[...] = jnp.zeros(halo_ref.shape, F32)

    def project(c):
        st = stage_refs[c % nst]
        for i, off in enumerate((c * ch, dff + c * ch)):
            st[i, HALO:HALO + tm, :] = jnp.dot(a2, wup_ref[:, off:off + ch], preferred_element_type=F32)

    def gate(c):
        st = stage_refs[c % nst]
        ys = []
        for i, off in enumerate((c * ch, dff + c * ch)):
            cols = slice(off, off + ch)
            st[i, 0:HALO, :] = halo_ref[:, cols]
            y = cb_ref[:, cols]
            for j in range(FFN_CONV):
                o = HALO - (FFN_CONV - 1) + j
                y = y + cw_ref[j:j + 1, cols] * st[i, o:o + tm, :]
            halo_ref[:, cols] = st[i, tm:tm + HALO, :]
            ys.append(y.astype(BF16))
        sig = 1.0 / (1.0 + jnp.exp2(ys[0] * (-LOG2E)))
        return ys[0] * sig * ys[1]

    nch = dff // ch
    out_ref[...] = h1
    project(0)
    project(1)
    act = gate(0)
    for c in range(nch):
        if c + 2 < nch:
            project(c + 2)
        act_next = gate(c + 1) if c + 1 < nch else None
        out_ref[...] += jnp.dot(act, wdown_ref[c * ch:(c + 1) * ch, :], preferred_element_type=F32)
        act = act_next
    out_ref[...] = _rms(out_ref[...], g3_ref[...])


def _const_spec(shape):
    return pl.BlockSpec(shape, lambda *_: (0,) * len(shape))


def kernel(x, norm_mix_g, w_in, b_gates, mlstm_conv_w, mlstm_conv_b, att_out_g, mlstm_out_g, w_out, norm_ffn_g, w_up, ffn_conv_w, ffn_conv_b, w_down, norm_final_g):
    B, S, D = x.shape
    H, dh = ATT_HEADS, ATT_HEAD_DIM
    wa = H * dh
    vp = dh + MOBA_VPAD
    wm = MLSTM_HEADS * MLSTM_HEAD_DIM
    L = MOBA_BLOCK
    nb = S // L
    assert S % L == 0 and nb <= 8 and MLSTM_CHUNK == L
    assert w_in.shape[0] == 1 and w_in.shape[2] == 3 * wa + 4 * wm + 2 * MLSTM_HEADS
    dff = w_down.shape[1]
    assert dff % FFN_COL_CHUNK == 0

    wi = w_in[0]
    o_mq, o_mv, o_g = 3 * wa, 3 * wa + 2 * wm, 3 * wa + 4 * wm
    wn = jnp.concatenate([wi[:, wa:2 * wa], wi[:, o_mq:o_mv]], axis=1).astype(BF16)
    wt = jnp.concatenate([wi[:, 0:wa], wi[:, 2 * wa:3 * wa], wi[:, o_mv:o_g]], axis=1).T.astype(BF16)
    nh = MLSTM_HEADS
    wgT = jnp.zeros((16, D), F32).at[0:nh].set(wi[:, o_g:o_g + nh].T).at[8:8 + nh].set(wi[:, o_g + nh:].T).astype(BF16)
    bgT = jnp.zeros((16,), F32).at[0:nh].set(b_gates[0, 0:nh]).at[8:8 + nh].set(b_gates[0, nh:])
    tm = INPROJ_ROW_TILE
    assert S % tm == 0 and tm % L == 0
    bgT = jnp.broadcast_to(bgT.reshape(16, 1), (16, tm))
    g1 = norm_mix_g[0].reshape(1, D)
    cw1 = mlstm_conv_w[0]
    cb1 = mlstm_conv_b[0].reshape(1, 2 * wm)

    nbt = tm // L
    params = pltpu.CompilerParams(dimension_semantics=("arbitrary", "arbitrary"), vmem_limit_bytes=VMEM_LIMIT)
    tile = lambda w: pl.BlockSpec((None, tm, w), lambda b, t: (b, t, 0))
    btile = lambda r, c: pl.BlockSpec((None, nbt, r, c), lambda b, t: (b, t, 0, 0))
    bshape = lambda r, c, dt: jax.ShapeDtypeStruct((B, nb, r, c), dt)
    qT, kb, kmean, vTb, mq, mk, mvT, moT, gT = pl.pallas_call(
        _inproj_kernel,
        grid=(B, S // tm),
        in_specs=[tile(D), _const_spec((1, D)), _const_spec((D, wa + 2 * wm)), _const_spec((2 * wa + 2 * wm, D)),
                  _const_spec((16, D)), _const_spec((16, tm)),
                  _const_spec((MLSTM_CONV, 2 * wm)), _const_spec((1, 2 * wm))],
        out_specs=[pl.BlockSpec((None, wa, tm), lambda b, t: (b, 0, t)),
                   btile(L, wa), btile(1, wa), btile(H * vp, L),
                   tile(wm), tile(wm), btile(wm, L), btile(wm, L), btile(16, L)],
        out_shape=[jax.ShapeDtypeStruct((B, wa, S), BF16),
                   bshape(L, wa, BF16), bshape(1, wa, F32), bshape(H * vp, L, BF16),
                   jax.ShapeDtypeStruct((B, S, wm), BF16),
                   jax.ShapeDtypeStruct((B, S, wm), BF16),
                   bshape(wm, L, BF16), bshape(wm, L, BF16), bshape(16, L, F32)],
        scratch_shapes=[pltpu.VMEM((HALO + tm, 2 * wm), F32)],
        compiler_params=params,
    )(x, g1, wn, wt, wgT, bgT, cw1, cb1)

    km = kmean.reshape(B, nb, H, dh)
    km = jnp.pad(km, ((0, 0), (0, 8 - nb), (0, 0), (0, 0)))
    eye = jnp.eye(H, dtype=F32)
    kmt = jnp.einsum('bjhd,hg->bhjgd', km, eye).reshape(B, H * 8, wa)
    kmh = kmt.astype(BF16)
    kml = (kmt - kmh.astype(F32)).astype(BF16)
    ga = att_out_g[0].reshape(wa, 1)

    att = pl.pallas_call(
        _moba_kernel,
        grid=(B, nb),
        in_specs=[pl.BlockSpec((None, wa, L), lambda b, i: (b, 0, i)),
                  pl.BlockSpec((None, nb, L, wa), lambda b, i: (b, 0, 0, 0)),
                  pl.BlockSpec((None, nb, H * vp, L), lambda b, i: (b, 0, 0, 0)),
                  pl.BlockSpec((None, H * 8, wa), lambda b, i: (b, 0, 0)),
                  pl.BlockSpec((None, H * 8, wa), lambda b, i: (b, 0, 0)),
                  _const_spec((wa, 1))],
        out_specs=pl.BlockSpec((None, L, wa), lambda b, i: (b, i, 0)),
        out_shape=jax.ShapeDtypeStruct((B, S, wa), BF16),
        scratch_shapes=[pltpu.VMEM((H * 2 * dh, L), BF16), pltpu.VMEM((H * 8, L), F32), pltpu.VMEM((wa, L), F32),
                        pltpu.VMEM((H * vp, L), F32), pltpu.VMEM((H, L), F32), pltpu.VMEM((2, H, L), F32),
                        pltpu.VMEM((2, H, L), F32), pltpu.VMEM((2, H, L, L), F32)],
        compiler_params=params,
    )(qT, kb, vTb, kmh, kml, ga)

    seq = lambda w: pl.BlockSpec((None, S, w), lambda b: (b, 0, 0))
    blocks = lambda r: pl.BlockSpec((None, nb, r, L), lambda b: (b, 0, 0, 0))
    gn = jnp.broadcast_to(mlstm_out_g[0].reshape(wm, 1), (wm, L))
    mh = pl.pallas_call(
        _mlstm_kernel,
        grid=(B,),
        in_specs=[seq(wm), seq(wm), blocks(wm), blocks(wm), blocks(16), _const_spec((wm, L))],
        out_specs=seq(wm),
        out_shape=jax.ShapeDtypeStruct((B, S, wm), BF16),
        scratch_shapes=[pltpu.VMEM((MLSTM_HEADS, MLSTM_HEAD_DIM + 8, MLSTM_HEAD_DIM), F32),
                        pltpu.VMEM((MLSTM_HEADS, L, L), F32), pltpu.VMEM((wm, L), F32),
                        pltpu.VMEM((nb, 8, L), F32), pltpu.VMEM((nb, 8, L), F32), pltpu.VMEM((nb, L, 128), F32)],
        compiler_params=pltpu.CompilerParams(dimension_semantics=("arbitrary",), vmem_limit_bytes=VMEM_LIMIT),
    )(mq, mk, mvT, moT, gT, gn)

    tf = FFN_ROW_TILE
    assert S % tf == 0
    ftile = lambda w: pl.BlockSpec((None, tf, w), lambda b, t: (b, t, 0))
    out = pl.pallas_call(
        _ffn_kernel,
        grid=(B, S // tf),
        in_specs=[ftile(D), ftile(wa), ftile(wm), _const_spec((wa + wm, D)), _const_spec((1, D)),
                  _const_spec((D, 2 * dff)), _const_spec((FFN_CONV, 2 * dff)), _const_spec((1, 2 * dff)),
                  _const_spec((dff, D)), _const_spec((1, D))],
        out_specs=ftile(D),
        out_shape=jax.ShapeDtypeStruct((B, S, D), x.dtype),
        scratch_shapes=[pltpu.VMEM((HALO, 2 * dff), F32)]
        + [pltpu.VMEM((2, HALO + tf, FFN_COL_CHUNK), F32) for _ in range(FFN_STAGES)],
        compiler_params=params,
    )(x, att, mh, w_out[0].astype(BF16), norm_ffn_g[0].reshape(1, D), w_up[0].astype(BF16), ffn_conv_w[0],
      ffn_conv_b[0].reshape(1, 2 * dff), w_down[0].astype(BF16), norm_final_g.reshape(1, D))
    return out
```

```python
import functools

import jax
import jax.numpy as jnp
from jax import lax
from jax.experimental import pallas as pl
from jax.experimental.pallas import tpu as pltpu

EPS = 1e-6
ATT_HEADS = 8
ATT_HEAD_DIM = 64
MOBA_BLOCK = 256
MOBA_TOPK = 3
MLSTM_HEADS = 4
MLSTM_HEAD_DIM = 128
MLSTM_CONV = 4
FFN_CONV = 3
MLSTM_CHUNK = MOBA_BLOCK
INPROJ_ROW_TILE = 512
INPROJ_CONV_CHUNK = 256
HALO = 8
NEG = -1e30
MOBA_RUN = 8
MOBA_VPAD = 16
LOG2E = 1.4426950408889634
FFN_COL_CHUNK = 256
FFN_ROW_TILE = 512
FFN_STAGES = 3
VMEM_LIMIT = 56 * 1024 * 1024

F32 = jnp.float32
BF16 = jnp.bfloat16

_NT = (((1,), (1,)), ((), ()))
_TN = (((0,), (0,)), ((), ()))


def _sigmoid(y):
    return 1.0 / (1.0 + jnp.exp(-y))


def _rms(x, g):
    return x * lax.rsqrt(jnp.mean(x * x, axis=-1, keepdims=True) + EPS) * g


def _inproj_kernel(x_ref, g_ref, wn_ref, wt_ref, wgT_ref, bgT_ref, cw_ref, cb_ref,
                   qT_ref, k_ref, kmean_ref, vT_ref, mq_ref, mk_ref, mvT_ref, moT_ref, gT_ref,
                   conv_ref):
    t = pl.program_id(1)
    tm = x_ref.shape[0]
    L = MOBA_BLOCK
    wa = k_ref.shape[2]
    wm = mq_ref.shape[1]

    @pl.when(t == 0)
    def _():
        conv_ref[0:HALO, :] = jnp.zeros((HALO, 2 * wm), F32)

    a = _rms(x_ref[...], g_ref[...]).astype(BF16)
    cc = INPROJ_CONV_CHUNK

    def project_conv(j):
        cols = slice(j * cc, (j + 1) * cc)
        conv_ref[HALO:HALO + tm, cols] = jnp.dot(a, wn_ref[:, wa + j * cc:wa + (j + 1) * cc],
                                                 preferred_element_type=F32)

    def finish_conv(j):
        cols = slice(j * cc, (j + 1) * cc)
        y = cb_ref[:, cols]
        for tap in range(MLSTM_CONV):
            off = HALO - (MLSTM_CONV - 1) + tap
            y = y + cw_ref[tap:tap + 1, cols] * conv_ref[off:off + tm, cols]
        conv_ref[0:HALO, cols] = conv_ref[tm:tm + HALO, cols]
        y = y.astype(BF16)
        y = y * (1.0 / (1.0 + jnp.exp2(y * (-LOG2E))))
        if (j + 1) * cc <= wm:
            mq_ref[:, cols] = y * (MLSTM_HEAD_DIM ** -0.5)
        else:
            mk_ref[:, j * cc - wm:(j + 1) * cc - wm] = y

    def project_k():
        k = jnp.dot(a, wn_ref[:, 0:wa], preferred_element_type=F32)
        for i in range(tm // L):
            blk = k[i * L:(i + 1) * L, :]
            k_ref[i] = blk.astype(BF16)
            kmean_ref[i] = jnp.mean(blk, axis=0, keepdims=True)

    def transposed(lo, hi):
        return lax.dot_general(wt_ref[lo:hi, :], a, _NT, preferred_element_type=F32)

    def project_qT():
        qT_ref[...] = (transposed(0, wa) * (ATT_HEAD_DIM ** -0.5 * LOG2E)).astype(BF16)

    def project_blocks(ref, lo, hi):
        vals = transposed(lo, hi)
        for i in range(tm // L):
            ref[i] = vals[:, i * L:(i + 1) * L].astype(BF16)

    def project_vT():
        vals = transposed(wa, 2 * wa)
        dh = ATT_HEAD_DIM
        vp = dh + MOBA_VPAD
        ones_row = (lax.broadcasted_iota(jnp.int32, (MOBA_VPAD, L), 0) == 0).astype(BF16)
        for i in range(tm // L):
            for h in range(ATT_HEADS):
                vT_ref[i, h * vp:h * vp + dh, :] = vals[h * dh:(h + 1) * dh, i * L:(i + 1) * L].astype(BF16)
                vT_ref[i, h * vp + dh:(h + 1) * vp, :] = ones_row

    def project_gates():
        gT = lax.dot_general(wgT_ref[...], a, _NT, preferred_element_type=F32) + bgT_ref[...]
        for i in range(tm // L):
            gT_ref[i] = gT[:, i * L:(i + 1) * L]

    others = [project_k, project_qT, project_vT,
              lambda: project_blocks(mvT_ref, 2 * wa, 2 * wa + wm),
              lambda: project_blocks(moT_ref, 2 * wa + wm, 2 * wa + 2 * wm), project_gates]
    nconv = 2 * wm // cc
    project_conv(0)
    for j in range(nconv):
        if j + 1 < nconv:
            project_conv(j + 1)
        if others:
            others.pop(0)()
        finish_conv(j)
    for f in others:
        f()


def _moba_kernel(qT_ref, k_ref, vT_ref, kmh_ref, kml_ref, g_ref, o_ref, qz_ref, bias_ref, oT_ref, acc_ref, m_ref,
                 alpha_ref, sub_ref, s_ref):
    i = pl.program_id(1)
    L = MOBA_BLOCK
    dh = ATT_HEAD_DIM
    nbp = 8
    def prepare():
        qT = qT_ref[...]
        zeros = jnp.zeros((dh, L), BF16)
        for h in range(ATT_HEADS):
            qh = qT[h * dh:(h + 1) * dh, :]
            lo, hi = (qh, zeros) if h % 2 == 0 else (zeros, qh)
            qz_ref[h * 2 * dh:h * 2 * dh + dh, :] = lo
            qz_ref[h * 2 * dh + dh:(h + 1) * 2 * dh, :] = hi
        gate = (jnp.dot(kmh_ref[...], qT, preferred_element_type=F32)
                + jnp.dot(kml_ref[...], qT, preferred_element_type=F32))
        jrow = lax.broadcasted_iota(jnp.int32, (nbp, L), 0)
        past = jrow < i
        for h in range(ATT_HEADS):
            g = gate[h * nbp:(h + 1) * nbp, :]
            gm = jnp.where(past, g, -jnp.inf)
            rank = jnp.zeros((nbp, L), jnp.int32)
            for ii in range(nbp):
                gi = gm[ii:ii + 1, :]
                beats = jnp.where(jrow > ii, jnp.where(gi >= g, 1, 0), jnp.where(gi > g, 1, 0))
                rank = rank + beats
            sel = past & (rank < MOBA_TOPK)
            bias_ref[h * nbp:(h + 1) * nbp, :] = jnp.where(sel, 0.0, NEG)

    krow = lax.broadcasted_iota(jnp.int32, (L, L), 0)
    qcol = lax.broadcasted_iota(jnp.int32, (L, L), 1)
    causal = krow <= qcol

    vp = dh + MOBA_VPAD

    def scores(j, slot, own):
        for p in range(ATT_HEADS // 2):
            k_j = k_ref[j, :, p * 2 * dh:(p + 1) * 2 * dh]
            for h in (2 * p, 2 * p + 1):
                s = jnp.dot(k_j, qz_ref[h * 2 * dh:(h + 1) * 2 * dh, :], preferred_element_type=F32)
                if own:
                    s = jnp.where(causal, s, NEG)
                    m_new = jnp.max(s, axis=0, keepdims=True)
                    alpha = jnp.zeros((1, L), F32)
                    sub = m_new
                else:
                    b = bias_ref[pl.ds(h * nbp + j, 1), :]
                    m = m_ref[h:h + 1, :]
                    m_new = jnp.maximum(m, jnp.max(s, axis=0, keepdims=True) + b)
                    alpha = jnp.exp2(m - m_new)
                    sub = m_new - b
                s_ref[slot, h] = s
                m_ref[h:h + 1, :] = m_new
                alpha_ref[slot, pl.ds(h, 1), :] = alpha
                sub_ref[slot, pl.ds(h, 1), :] = sub

    def values(j, slot):
        for h in range(ATT_HEADS):
            rows = slice(h * vp, (h + 1) * vp)
            pr = jnp.exp2(s_ref[slot, h] - sub_ref[slot, pl.ds(h, 1), :])
            pv = jnp.dot(vT_ref[j, rows, :], pr.astype(BF16), preferred_element_type=F32)
            acc_ref[rows, :] = alpha_ref[slot, pl.ds(h, 1), :] * acc_ref[rows, :] + pv

    def visit_run(blocks):
        scores(blocks[0][0], 0, blocks[0][1])
        for r, (j, _) in enumerate(blocks):
            if r + 1 < len(blocks):
                scores(blocks[r + 1][0], (r + 1) % 2, blocks[r + 1][1])
            values(j, r % 2)

    acc_ref[...] = jnp.zeros(acc_ref.shape, F32)
    tail = lax.rem(i, MOBA_RUN)
    for n in range(MOBA_RUN):
        @pl.when(tail == n)
        def _(n=n):
            prepare()
            visit_run([(i, True)] + [(r, False) for r in range(n)])

    def full_run(t, carry):
        visit_run([(tail + MOBA_RUN * t + r, False) for r in range(MOBA_RUN)])
        return carry

    lax.fori_loop(0, i // MOBA_RUN, full_run, 0)

    for h in range(ATT_HEADS):
        rows = slice(h * dh, (h + 1) * dh)
        acc = acc_ref[h * vp:(h + 1) * vp, :]
        o = acc[0:dh, :] * (1.0 / acc[dh:dh + 1, :])
        oT_ref[rows, :] = o * lax.rsqrt(jnp.mean(o * o, axis=0, keepdims=True) + EPS) * g_ref[rows, :]

    o_ref[...] = oT_ref[...].T.astype(BF16)


def _split3(x):
    hi = x.astype(BF16)
    r = x - hi.astype(F32)
    mid = r.astype(BF16)
    lo = (r - mid.astype(F32)).astype(BF16)
    return hi, mid, lo


def _mlstm_kernel(q_ref, k_ref, vT_ref, oT_ref, gT_ref, gn_ref, out_ref, c_ref, s_ref, hT_ref, b_ref, crow_ref,
                  ccol_ref):
    nc, _, L = vT_ref.shape
    D = MLSTM_HEAD_DIM
    H = MLSTM_HEADS
    c_ref[...] = jnp.zeros(c_ref.shape, F32)
    srow = lax.broadcasted_iota(jnp.int32, (L, L), 0)
    tcol = lax.broadcasted_iota(jnp.int32, (L, L), 1)
    causal = srow <= tcol
    upper = causal.astype(BF16)
    first_row = lax.broadcasted_iota(jnp.int32, (8, L), 0) == 0

    for c in range(nc):
        gT = gT_ref[c]
        lfT = (jnp.minimum(gT, 0.0) - jnp.log(1.0 + jnp.exp(-jnp.abs(gT)))) * LOG2E
        bT = sum(jnp.dot(part, upper, preferred_element_type=F32) for part in _split3(lfT))
        c8 = gT[0:8, :] * LOG2E - bT[8:16, :]
        b_ref[c] = bT[8:16, :]
        crow_ref[c] = c8
        ccol_ref[c] = jnp.concatenate([c8, jnp.zeros((128 - 8, L), F32)], axis=0).T

    def chunk(c, ms):
        rows = pl.ds(pl.multiple_of(c * L, L), L)
        b8 = b_ref[c]
        c8 = crow_ref[c]
        c_cols = ccol_ref[c]
        inters = []
        for h in range(H):
            cols = slice(h * D, (h + 1) * D)
            q = q_ref[rows, cols]
            s_ref[h] = lax.dot_general(k_ref[rows, cols], q, _NT, preferred_element_type=F32)
            inters.append(lax.dot_general(c_ref[h].astype(BF16), q, _NT, preferred_element_type=F32))
        new_ms = []
        for h in range(H):
            cols = slice(h * D, (h + 1) * D)
            m_prev = ms[h]
            b_row = b8[h:h + 1, :]
            c_row = c8[h:h + 1, :]
            cm = jnp.where(causal, c_cols[:, h:h + 1], NEG)
            mx = jnp.maximum(m_prev, jnp.max(cm, axis=0, keepdims=True))
            w_inter = jnp.exp2(m_prev - mx)
            sT = s_ref[h] * jnp.exp2(cm - mx)
            vT = vT_ref[c, cols, :]
            inter = inters[h]
            num = w_inter * inter[0:D, :] + jnp.dot(vT, sT.astype(BF16), preferred_element_type=F32)
            den = w_inter * inter[D:D + 1, :] + jnp.sum(sT, axis=0, keepdims=True)
            m_t = b_row + mx
            hid = num * (1.0 / jnp.maximum(jnp.abs(den), jnp.exp2(-m_t)))
            hid = hid * lax.rsqrt(jnp.mean(hid * hid, axis=0, keepdims=True) + EPS) * gn_ref[cols, :]
            hT_ref[cols, :] = hid * _sigmoid(oT_ref[c, cols, :].astype(F32))

            m_last = m_t[:, L - 1:L]
            decay = w_inter[:, L - 1:L]
            w_last = jnp.exp2(c_row + (b_row[:, L - 1:L] - m_last))
            lhs = jnp.concatenate([vT.astype(F32) * w_last,
                                   jnp.where(first_row, w_last, 0.0)], axis=0).astype(BF16)
            c_ref[h] = decay * c_ref[h] + jnp.dot(lhs, k_ref[rows, cols], preferred_element_type=F32)
            new_ms.append(m_last)
        out_ref[rows, :] = hT_ref[...].T.astype(out_ref.dtype)
        return tuple(new_ms)

    lax.fori_loop(0, nc, chunk, tuple(jnp.zeros((1, 1), F32) for _ in range(H)))


def _ffn_kernel(x_ref, att_ref, mh_ref, wout_ref, g2_ref, wup_ref, cw_ref, cb_ref, wdown_ref, g3_ref,
                out_ref, halo_ref, *stage_refs):
    t = pl.program_id(1)
    tm = x_ref.shape[0]
    dff = wdown_ref.shape[0]
    ch = FFN_COL_CHUNK
    nst = len(stage_refs)

    heads = jnp.concatenate([att_ref[...], mh_ref[...]], axis=1)
    h1 = x_ref[...] + jnp.dot(heads, wout_ref[...], preferred_element_type=F32)
    a2 = _rms(h1, g2_ref[...]).astype(BF16)

    @pl.when(t == 0)
    def _():
        halo_ref[...] = jnp.zeros(halo_ref.shape, F32)

    def project(c):
        st = stage_refs[c % nst]
        for i, off in enumerate((c * ch, dff + c * ch)):
            st[i, HALO:HALO + tm, :] = jnp.dot(a2, wup_ref[:, off:off + ch], preferred_element_type=F32)

    def gate(c):
        st = stage_refs[c % nst]
        ys = []
        for i, off in enumerate((c * ch, dff + c * ch)):
            cols = slice(off, off + ch)
            st[i, 0:HALO, :] = halo_ref[:, cols]
            y = cb_ref[:, cols]
            for j in range(FFN_CONV):
                o = HALO - (FFN_CONV - 1) + j
                y = y + cw_ref[j:j + 1, cols] * st[i, o:o + tm, :]
            halo_ref[:, cols] = st[i, tm:tm + HALO, :]
            ys.append(y.astype(BF16))
        sig = 1.0 / (1.0 + jnp.exp2(ys[0] * (-LOG2E)))
        return ys[0] * sig * ys[1]

    nch = dff // ch
    out_ref[...] = h1
    project(0)
    project(1)
    act = gate(0)
    for c in range(nch):
        if c + 2 < nch:
            project(c + 2)
        act_next = gate(c + 1) if c + 1 < nch else None
        out_ref[...] += jnp.dot(act, wdown_ref[c * ch:(c + 1) * ch, :], preferred_element_type=F32)
        act = act_next
    out_ref[...] = _rms(out_ref[...], g3_ref[...])


def _const_spec(shape):
    return pl.BlockSpec(shape, lambda *_: (0,) * len(shape))


def kernel(x, norm_mix_g, w_in, b_gates, mlstm_conv_w, mlstm_conv_b, att_out_g, mlstm_out_g, w_out, norm_ffn_g, w_up, ffn_conv_w, ffn_conv_b, w_down, norm_final_g):
    B, S, D = x.shape
    H, dh = ATT_HEADS, ATT_HEAD_DIM
    wa = H * dh
    vp = dh + MOBA_VPAD
    wm = MLSTM_HEADS * MLSTM_HEAD_DIM
    L = MOBA_BLOCK
    nb = S // L
    assert S % L == 0 and nb <= 8 and MLSTM_CHUNK == L
    assert w_in.shape[0] == 1 and w_in.shape[2] == 3 * wa + 4 * wm + 2 * MLSTM_HEADS
    dff = w_down.shape[1]
    assert dff % FFN_COL_CHUNK == 0

    wi = w_in[0]
    o_mq, o_mv, o_g = 3 * wa, 3 * wa + 2 * wm, 3 * wa + 4 * wm
    wn = jnp.concatenate([wi[:, wa:2 * wa], wi[:, o_mq:o_mv]], axis=1).astype(BF16)
    wt = jnp.concatenate([wi[:, 0:wa], wi[:, 2 * wa:3 * wa], wi[:, o_mv:o_g]], axis=1).T.astype(BF16)
    nh = MLSTM_HEADS
    wgT = jnp.zeros((16, D), F32).at[0:nh].set(wi[:, o_g:o_g + nh].T).at[8:8 + nh].set(wi[:, o_g + nh:].T).astype(BF16)
    bgT = jnp.zeros((16,), F32).at[0:nh].set(b_gates[0, 0:nh]).at[8:8 + nh].set(b_gates[0, nh:])
    tm = INPROJ_ROW_TILE
    assert S % tm == 0 and tm % L == 0
    bgT = jnp.broadcast_to(bgT.reshape(16, 1), (16, tm))
    g1 = norm_mix_g[0].reshape(1, D)
    cw1 = mlstm_conv_w[0]
    cb1 = mlstm_conv_b[0].reshape(1, 2 * wm)

    nbt = tm // L
    params = pltpu.CompilerParams(dimension_semantics=("arbitrary", "arbitrary"), vmem_limit_bytes=VMEM_LIMIT)
    tile = lambda w: pl.BlockSpec((None, tm, w), lambda b, t: (b, t, 0))
    btile = lambda r, c: pl.BlockSpec((None, nbt, r, c), lambda b, t: (b, t, 0, 0))
    bshape = lambda r, c, dt: jax.ShapeDtypeStruct((B, nb, r, c), dt)
    qT, kb, kmean, vTb, mq, mk, mvT, moT, gT = pl.pallas_call(
        _inproj_kernel,
        grid=(B, S // tm),
        in_specs=[tile(D), _const_spec((1, D)), _const_spec((D, wa + 2 * wm)), _const_spec((2 * wa + 2 * wm, D)),
                  _const_spec((16, D)), _const_spec((16, tm)),
                  _const_spec((MLSTM_CONV, 2 * wm)), _const_spec((1, 2 * wm))],
        out_specs=[pl.BlockSpec((None, wa, tm), lambda b, t: (b, 0, t)),
                   btile(L, wa), btile(1, wa), btile(H * vp, L),
                   tile(wm), tile(wm), btile(wm, L), btile(wm, L), btile(16, L)],
        out_shape=[jax.ShapeDtypeStruct((B, wa, S), BF16),
                   bshape(L, wa, BF16), bshape(1, wa, F32), bshape(H * vp, L, BF16),
                   jax.ShapeDtypeStruct((B, S, wm), BF16),
                   jax.ShapeDtypeStruct((B, S, wm), BF16),
                   bshape(wm, L, BF16), bshape(wm, L, BF16), bshape(16, L, F32)],
        scratch_shapes=[pltpu.VMEM((HALO + tm, 2 * wm), F32)],
        compiler_params=params,
    )(x, g1, wn, wt, wgT, bgT, cw1, cb1)

    km = kmean.reshape(B, nb, H, dh)
    km = jnp.pad(km, ((0, 0), (0, 8 - nb), (0, 0), (0, 0)))
    eye = jnp.eye(H, dtype=F32)
    kmt = jnp.einsum('bjhd,hg->bhjgd', km, eye).reshape(B, H * 8, wa)
    kmh = kmt.astype(BF16)
    kml = (kmt - kmh.astype(F32)).astype(BF16)
    ga = att_out_g[0].reshape(wa, 1)

    att = pl.pallas_call(
        _moba_kernel,
        grid=(B, nb),
        in_specs=[pl.BlockSpec((None, wa, L), lambda b, i: (b, 0, i)),
                  pl.BlockSpec((None, nb, L, wa), lambda b, i: (b, 0, 0, 0)),
                  pl.BlockSpec((None, nb, H * vp, L), lambda b, i: (b, 0, 0, 0)),
                  pl.BlockSpec((None, H * 8, wa), lambda b, i: (b, 0, 0)),
                  pl.BlockSpec((None, H * 8, wa), lambda b, i: (b, 0, 0)),
                  _const_spec((wa, 1))],
        out_specs=pl.BlockSpec((None, L, wa), lambda b, i: (b, i, 0)),
        out_shape=jax.ShapeDtypeStruct((B, S, wa), BF16),
        scratch_shapes=[pltpu.VMEM((H * 2 * dh, L), BF16), pltpu.VMEM((H * 8, L), F32), pltpu.VMEM((wa, L), F32),
                        pltpu.VMEM((H * vp, L), F32), pltpu.VMEM((H, L), F32), pltpu.VMEM((2, H, L), F32),
                        pltpu.VMEM((2, H, L), F32), pltpu.VMEM((2, H, L, L), F32)],
        compiler_params=params,
    )(qT, kb, vTb, kmh, kml, ga)

    seq = lambda w: pl.BlockSpec((None, S, w), lambda b: (b, 0, 0))
    blocks = lambda r: pl.BlockSpec((None, nb, r, L), lambda b: (b, 0, 0, 0))
    gn = jnp.broadcast_to(mlstm_out_g[0].reshape(wm, 1), (wm, L))
    mh = pl.pallas_call(
        _mlstm_kernel,
        grid=(B,),
        in_specs=[seq(wm), seq(wm), blocks(wm), blocks(wm), blocks(16), _const_spec((wm, L))],
        out_specs=seq(wm),
        out_shape=jax.ShapeDtypeStruct((B, S, wm), BF16),
        scratch_shapes=[pltpu.VMEM((MLSTM_HEADS, MLSTM_HEAD_DIM + 8, MLSTM_HEAD_DIM), F32),
                        pltpu.VMEM((MLSTM_HEADS, L, L), F32), pltpu.VMEM((wm, L), F32),
                        pltpu.VMEM((nb, 8, L), F32), pltpu.VMEM((nb, 8, L), F32), pltpu.VMEM((nb, L, 128), F32)],
        compiler_params=pltpu.CompilerParams(dimension_semantics=("arbitrary",), vmem_limit_bytes=VMEM_LIMIT),
    )(mq, mk, mvT, moT, gT, gn)

    tf = FFN_ROW_TILE
    assert S % tf == 0
    ftile = lambda w: pl.BlockSpec((None, tf, w), lambda b, t: (b, t, 0))
    out = pl.pallas_call(
        _ffn_kernel,
        grid=(B, S // tf),
        in_specs=[ftile(D), ftile(wa), ftile(wm), _const_spec((wa + wm, D)), _const_spec((1, D)),
                  _const_spec((D, 2 * dff)), _const_spec((FFN_CONV, 2 * dff)), _const_spec((1, 2 * dff)),
                  _const_spec((dff, D)), _const_spec((1, D))],
        out_specs=ftile(D),
        out_shape=jax.ShapeDtypeStruct((B, S, D), x.dtype),
        scratch_shapes=[pltpu.VMEM((HALO, 2 * dff), F32)]
        + [pltpu.VMEM((2, HALO + tf, FFN_COL_CHUNK), F32) for _ in range(FFN_STAGES)],
        compiler_params=params,
    )(x, att, mh, w_out[0].astype(BF16), norm_ffn_g[0].reshape(1, D), w_up[0].astype(BF16), ffn_conv_w[0],
      ffn_conv_b[0].reshape(1, 2 * dff), w_down[0].astype(BF16), norm_final_g.reshape(1, D))
    return out
```

```python
import functools

import jax
import jax.numpy as jnp
from jax import lax
from jax.experimental import pallas as pl
from jax.experimental.pallas import tpu as pltpu

EPS = 1e-6
ATT_HEADS = 8
ATT_HEAD_DIM = 64
MOBA_BLOCK = 256
MOBA_TOPK = 3
MLSTM_HEADS = 4
MLSTM_HEAD_DIM = 128
MLSTM_CONV = 4
FFN_CONV = 3
MLSTM_CHUNK = MOBA_BLOCK
INPROJ_ROW_TILE = 512
INPROJ_CONV_CHUNK = 256
HALO = 8
NEG = -1e30
MOBA_RUN = 4
MOBA_VPAD = 16
LOG2E = 1.4426950408889634
FFN_COL_CHUNK = 256
FFN_ROW_TILE = 512
FFN_STAGES = 3
VMEM_LIMIT = 56 * 1024 * 1024

F32 = jnp.float32
BF16 = jnp.bfloat16

_NT = (((1,), (1,)), ((), ()))
_TN = (((0,), (0,)), ((), ()))


def _sigmoid(y):
    return 1.0 / (1.0 + jnp.exp(-y))


def _rms(x, g):
    return x * lax.rsqrt(jnp.mean(x * x, axis=-1, keepdims=True) + EPS) * g


def _inproj_kernel(x_ref, g_ref, wn_ref, wt_ref, wgT_ref, bgT_ref, cw_ref, cb_ref,
                   qT_ref, k_ref, kmean_ref, vT_ref, mq_ref, mk_ref, mvT_ref, moT_ref, gT_ref,
                   conv_ref):
    t = pl.program_id(1)
    tm = x_ref.shape[0]
    L = MOBA_BLOCK
    wa = k_ref.shape[2]
    wm = mq_ref.shape[1]

    @pl.when(t == 0)
    def _():
        conv_ref[0:HALO, :] = jnp.zeros((HALO, 2 * wm), F32)

    a = _rms(x_ref[...], g_ref[...]).astype(BF16)
    cc = INPROJ_CONV_CHUNK

    def project_conv(j):
        cols = slice(j * cc, (j + 1) * cc)
        conv_ref[HALO:HALO + tm, cols] = jnp.dot(a, wn_ref[:, wa + j * cc:wa + (j + 1) * cc],
                                                 preferred_element_type=F32)

    def finish_conv(j):
        cols = slice(j * cc, (j + 1) * cc)
        y = cb_ref[:, cols]
        for tap in range(MLSTM_CONV):
            off = HALO - (MLSTM_CONV - 1) + tap
            y = y + cw_ref[tap:tap + 1, cols] * conv_ref[off:off + tm, cols]
        conv_ref[0:HALO, cols] = conv_ref[tm:tm + HALO, cols]
        y = y.astype(BF16)
        y = y * (1.0 / (1.0 + jnp.exp2(y * (-LOG2E))))
        if (j + 1) * cc <= wm:
            mq_ref[:, cols] = y * (MLSTM_HEAD_DIM ** -0.5)
        else:
            mk_ref[:, j * cc - wm:(j + 1) * cc - wm] = y

    def project_k():
        k = jnp.dot(a, wn_ref[:, 0:wa], preferred_element_type=F32)
        for i in range(tm // L):
            blk = k[i * L:(i + 1) * L, :]
            k_ref[i] = blk.astype(BF16)
            kmean_ref[i] = jnp.mean(blk, axis=0, keepdims=True)

    def transposed(lo, hi):
        return lax.dot_general(wt_ref[lo:hi, :], a, _NT, preferred_element_type=F32)

    def project_qT():
        qT_ref[...] = (transposed(0, wa) * (ATT_HEAD_DIM ** -0.5 * LOG2E)).astype(BF16)

    def project_blocks(ref, lo, hi):
        vals = transposed(lo, hi)
        for i in range(tm // L):
            ref[i] = vals[:, i * L:(i + 1) * L].astype(BF16)

    def project_vT():
        vals = transposed(wa, 2 * wa)
        dh = ATT_HEAD_DIM
        vp = dh + MOBA_VPAD
        ones_row = (lax.broadcasted_iota(jnp.int32, (MOBA_VPAD, L), 0) == 0).astype(BF16)
        for i in range(tm // L):
            for h in range(ATT_HEADS):
                vT_ref[i, h * vp:h * vp + dh, :] = vals[h * dh:(h + 1) * dh, i * L:(i + 1) * L].astype(BF16)
                vT_ref[i, h * vp + dh:(h + 1) * vp, :] = ones_row

    def project_gates():
        gT = lax.dot_general(wgT_ref[...], a, _NT, preferred_element_type=F32) + bgT_ref[...]
        for i in range(tm // L):
            gT_ref[i] = gT[:, i * L:(i + 1) * L]

    others = [project_k, project_qT, project_vT,
              lambda: project_blocks(mvT_ref, 2 * wa, 2 * wa + wm),
              lambda: project_blocks(moT_ref, 2 * wa + wm, 2 * wa + 2 * wm), project_gates]
    nconv = 2 * wm // cc
    project_conv(0)
    for j in range(nconv):
        if j + 1 < nconv:
            project_conv(j + 1)
        if others:
            others.pop(0)()
        finish_conv(j)
    for f in others:
        f()


def _moba_kernel(qT_ref, k_ref, vT_ref, kmh_ref, kml_ref, g_ref, o_ref, qz_ref, bias_ref, oT_ref, acc_ref, m_ref,
                 alpha_ref, sub_ref, s_ref):
    i = pl.program_id(1)
    L = MOBA_BLOCK
    dh = ATT_HEAD_DIM
    nbp = 8
    def prepare():
        qT = qT_ref[...]
        zeros = jnp.zeros((dh, L), BF16)
        for h in range(ATT_HEADS):
            qh = qT[h * dh:(h + 1) * dh, :]
            lo, hi = (qh, zeros) if h % 2 == 0 else (zeros, qh)
            qz_ref[h * 2 * dh:h * 2 * dh + dh, :] = lo
            qz_ref[h * 2 * dh + dh:(h + 1) * 2 * dh, :] = hi
        gate = (jnp.dot(kmh_ref[...], qT, preferred_element_type=F32)
                + jnp.dot(kml_ref[...], qT, preferred_element_type=F32))
        jrow = lax.broadcasted_iota(jnp.int32, (nbp, L), 0)
        past = jrow < i
        for h in range(ATT_HEADS):
            g = gate[h * nbp:(h + 1) * nbp, :]
            gm = jnp.where(past, g, -jnp.inf)
            rank = jnp.zeros((nbp, L), jnp.int32)
            for ii in range(nbp):
                gi = gm[ii:ii + 1, :]
                beats = jnp.where(jrow > ii, jnp.where(gi >= g, 1, 0), jnp.where(gi > g, 1, 0))
                rank = rank + beats
            sel = past & (rank < MOBA_TOPK)
            bias_ref[h * nbp:(h + 1) * nbp, :] = jnp.where(sel, 0.0, NEG)

    krow = lax.broadcasted_iota(jnp.int32, (L, L), 0)
    qcol = lax.broadcasted_iota(jnp.int32, (L, L), 1)
    causal = krow <= qcol

    vp = dh + MOBA_VPAD

    def scores(j, slot, own):
        for p in range(ATT_HEADS // 2):
            k_j = k_ref[j, :, p * 2 * dh:(p + 1) * 2 * dh]
            for h in (2 * p, 2 * p + 1):
                s = jnp.dot(k_j, qz_ref[h * 2 * dh:(h + 1) * 2 * dh, :], preferred_element_type=F32)
                if own:
                    s = jnp.where(causal, s, NEG)
                    m_new = jnp.max(s, axis=0, keepdims=True)
                    alpha = jnp.zeros((1, L), F32)
                    sub = m_new
                else:
                    b = bias_ref[pl.ds(h * nbp + j, 1), :]
                    m = m_ref[h:h + 1, :]
                    m_new = jnp.maximum(m, jnp.max(s, axis=0, keepdims=True) + b)
                    alpha = jnp.exp2(m - m_new)
                    sub = m_new - b
                s_ref[slot, h] = s
                m_ref[h:h + 1, :] = m_new
                alpha_ref[slot, pl.ds(h, 1), :] = alpha
                sub_ref[slot, pl.ds(h, 1), :] = sub

    def values(j, slot):
        for h in range(ATT_HEADS):
            rows = slice(h * vp, (h + 1) * vp)
            pr = jnp.exp2(s_ref[slot, h] - sub_ref[slot, pl.ds(h, 1), :])
            pv = jnp.dot(vT_ref[j, rows, :], pr.astype(BF16), preferred_element_type=F32)
            acc_ref[rows, :] = alpha_ref[slot, pl.ds(h, 1), :] * acc_ref[rows, :] + pv

    def visit_run(blocks):
        scores(blocks[0][0], 0, blocks[0][1])
        for r, (j, _) in enumerate(blocks):
            if r + 1 < len(blocks):
                scores(blocks[r + 1][0], (r + 1) % 2, blocks[r + 1][1])
            values(j, r % 2)

    acc_ref[...] = jnp.zeros(acc_ref.shape, F32)
    tail = lax.rem(i, MOBA_RUN)
    for n in range(MOBA_RUN):
        @pl.when(tail == n)
        def _(n=n):
            prepare()
            visit_run([(i, True)] + [(r, False) for r in range(n)])

    def full_run(t, carry):
        visit_run([(tail + MOBA_RUN * t + r, False) for r in range(MOBA_RUN)])
        return carry

    lax.fori_loop(0, i // MOBA_RUN, full_run, 0)

    for h in range(ATT_HEADS):
        rows = slice(h * dh, (h + 1) * dh)
        acc = acc_ref[h * vp:(h + 1) * vp, :]
        o = acc[0:dh, :] * (1.0 / acc[dh:dh + 1, :])
        oT_ref[rows, :] = o * lax.rsqrt(jnp.mean(o * o, axis=0, keepdims=True) + EPS) * g_ref[rows, :]

    o_ref[...] = oT_ref[...].T.astype(BF16)


def _split3(x):
    hi = x.astype(BF16)
    r = x - hi.astype(F32)
    mid = r.astype(BF16)
    lo = (r - mid.astype(F32)).astype(BF16)
    return hi, mid, lo


def _mlstm_kernel(q_ref, k_ref, vT_ref, oT_ref, gT_ref, gn_ref, out_ref, c_ref, s_ref, hT_ref, b_ref, crow_ref,
                  ccol_ref):
    nc, _, L = vT_ref.shape
    D = MLSTM_HEAD_DIM
    H = MLSTM_HEADS
    c_ref[...] = jnp.zeros(c_ref.shape, F32)
    srow = lax.broadcasted_iota(jnp.int32, (L, L), 0)
    tcol = lax.broadcasted_iota(jnp.int32, (L, L), 1)
    causal = srow <= tcol
    upper = causal.astype(BF16)
    first_row = lax.broadcasted_iota(jnp.int32, (8, L), 0) == 0

    for c in range(nc):
        gT = gT_ref[c]
        lfT = (jnp.minimum(gT, 0.0) - jnp.log(1.0 + jnp.exp(-jnp.abs(gT)))) * LOG2E
        bT = sum(jnp.dot(part, upper, preferred_element_type=F32) for part in _split3(lfT))
        c8 = gT[0:8, :] * LOG2E - bT[8:16, :]
        b_ref[c] = bT[8:16, :]
        crow_ref[c] = c8
        ccol_ref[c] = jnp.concatenate([c8, jnp.zeros((128 - 8, L), F32)], axis=0).T

    def chunk(c, ms):
        rows = pl.ds(pl.multiple_of(c * L, L), L)
        b8 = b_ref[c]
        c8 = crow_ref[c]
        c_cols = ccol_ref[c]
        inters = []
        for h in range(H):
            cols = slice(h * D, (h + 1) * D)
            q = q_ref[rows, cols]
            s_ref[h] = lax.dot_general(k_ref[rows, cols], q, _NT, preferred_element_type=F32)
            inters.append(lax.dot_general(c_ref[h].astype(BF16), q, _NT, preferred_element_type=F32))
        new_ms = []
        for h in range(H):
            cols = slice(h * D, (h + 1) * D)
            m_prev = ms[h]
            b_row = b8[h:h + 1, :]
            c_row = c8[h:h + 1, :]
            cm = jnp.where(causal, c_cols[:, h:h + 1], NEG)
            mx = jnp.maximum(m_prev, jnp.max(cm, axis=0, keepdims=True))
            w_inter = jnp.exp2(m_prev - mx)
            sT = s_ref[h] * jnp.exp2(cm - mx)
            vT = vT_ref[c, cols, :]
            inter = inters[h]
            num = w_inter * inter[0:D, :] + jnp.dot(vT, sT.astype(BF16), preferred_element_type=F32)
            den = w_inter * inter[D:D + 1, :] + jnp.sum(sT, axis=0, keepdims=True)
            m_t = b_row + mx
            hid = num * (1.0 / jnp.maximum(jnp.abs(den), jnp.exp2(-m_t)))
            hid = hid * lax.rsqrt(jnp.mean(hid * hid, axis=0, keepdims=True) + EPS) * gn_ref[cols, :]
            hT_ref[cols, :] = hid * _sigmoid(oT_ref[c, cols, :].astype(F32))

            m_last = m_t[:, L - 1:L]
            decay = w_inter[:, L - 1:L]
            w_last = jnp.exp2(c_row + (b_row[:, L - 1:L] - m_last))
            lhs = jnp.concatenate([vT.astype(F32) * w_last,
                                   jnp.where(first_row, w_last, 0.0)], axis=0).astype(BF16)
            c_ref[h] = decay * c_ref[h] + jnp.dot(lhs, k_ref[rows, cols], preferred_element_type=F32)
            new_ms.append(m_last)
        out_ref[rows, :] = hT_ref[...].T.astype(out_ref.dtype)
        return tuple(new_ms)

    lax.fori_loop(0, nc, chunk, tuple(jnp.zeros((1, 1), F32) for _ in range(H)))


def _ffn_kernel(x_ref, att_ref, mh_ref, wout_ref, g2_ref, wup_ref, cw_ref, cb_ref, wdown_ref, g3_ref,
                out_ref, halo_ref, *stage_refs):
    t = pl.program_id(1)
    tm = x_ref.shape[0]
    dff = wdown_ref.shape[0]
    ch = FFN_COL_CHUNK
    nst = len(stage_refs)

    heads = jnp.concatenate([att_ref[...], mh_ref[...]], axis=1)
    h1 = x_ref[...] + jnp.dot(heads, wout_ref[...], preferred_element_type=F32)
    a2 = _rms(h1, g2_ref[...]).astype(BF16)

    @pl.when(t == 0)
    def _():
        halo_ref[...] = jnp.zeros(halo_ref.shape, F32)

    def project(c):
        st = stage_refs[c % nst]
        for i, off in enumerate((c * ch, dff + c * ch)):
            st[i, HALO:HALO + tm, :] = jnp.dot(a2, wup_ref[:, off:off + ch], preferred_element_type=F32)

    def gate(c):
        st = stage_refs[c % nst]
        ys = []
        for i, off in enumerate((c * ch, dff + c * ch)):
            cols = slice(off, off + ch)
            st[i, 0:HALO, :] = halo_ref[:, cols]
            y = cb_ref[:, cols]
            for j in range(FFN_CONV):
                o = HALO - (FFN_CONV - 1) + j
                y = y + cw_ref[j:j + 1, cols] * st[i, o:o + tm, :]
            halo_ref[:, cols] = st[i, tm:tm + HALO, :]
            ys.append(y.astype(BF16))
        sig = 1.0 / (1.0 + jnp.exp2(ys[0] * (-LOG2E)))
        return ys[0] * sig * ys[1]

    nch = dff // ch
    out_ref[...] = h1
    project(0)
    project(1)
    act = gate(0)
    for c in range(nch):
        if c + 2 < nch:
            project(c + 2)
        act_next = gate(c + 1) if c + 1 < nch else None
        out_ref[...] += jnp.dot(act, wdown_ref[c * ch:(c + 1) * ch, :], preferred_element_type=F32)
        act = act_next
    out_ref[...] = _rms(out_ref[...], g3_ref[...])


def _const_spec(shape):
    return pl.BlockSpec(shape, lambda *_: (0,) * len(shape))


def kernel(x, norm_mix_g, w_in, b_gates, mlstm_conv_w, mlstm_conv_b, att_out_g, mlstm_out_g, w_out, norm_ffn_g, w_up, ffn_conv_w, ffn_conv_b, w_down, norm_final_g):
    B, S, D = x.shape
    H, dh = ATT_HEADS, ATT_HEAD_DIM
    wa = H * dh
    vp = dh + MOBA_VPAD
    wm = MLSTM_HEADS * MLSTM_HEAD_DIM
    L = MOBA_BLOCK
    nb = S // L
    assert S % L == 0 and nb <= 8 and MLSTM_CHUNK == L
    assert w_in.shape[0] == 1 and w_in.shape[2] == 3 * wa + 4 * wm + 2 * MLSTM_HEADS
    dff = w_down.shape[1]
    assert dff % FFN_COL_CHUNK == 0

    wi = w_in[0]
    o_mq, o_mv, o_g = 3 * wa, 3 * wa + 2 * wm, 3 * wa + 4 * wm
    wn = jnp.concatenate([wi[:, wa:2 * wa], wi[:, o_mq:o_mv]], axis=1).astype(BF16)
    wt = jnp.concatenate([wi[:, 0:wa], wi[:, 2 * wa:3 * wa], wi[:, o_mv:o_g]], axis=1).T.astype(BF16)
    nh = MLSTM_HEADS
    wgT = jnp.zeros((16, D), F32).at[0:nh].set(wi[:, o_g:o_g + nh].T).at[8:8 + nh].set(wi[:, o_g + nh:].T).astype(BF16)
    bgT = jnp.zeros((16,), F32).at[0:nh].set(b_gates[0, 0:nh]).at[8:8 + nh].set(b_gates[0, nh:])
    tm = INPROJ_ROW_TILE
    assert S % tm == 0 and tm % L == 0
    bgT = jnp.broadcast_to(bgT.reshape(16, 1), (16, tm))
    g1 = norm_mix_g[0].reshape(1, D)
    cw1 = mlstm_conv_w[0]
    cb1 = mlstm_conv_b[0].reshape(1, 2 * wm)

    nbt = tm // L
    params = pltpu.CompilerParams(dimension_semantics=("arbitrary", "arbitrary"), vmem_limit_bytes=VMEM_LIMIT)
    tile = lambda w: pl.BlockSpec((None, tm, w), lambda b, t: (b, t, 0))
    btile = lambda r, c: pl.BlockSpec((None, nbt, r, c), lambda b, t: (b, t, 0, 0))
    bshape = lambda r, c, dt: jax.ShapeDtypeStruct((B, nb, r, c), dt)
    qT, kb, kmean, vTb, mq, mk, mvT, moT, gT = pl.pallas_call(
        _inproj_kernel,
        grid=(B, S // tm),
        in_specs=[tile(D), _const_spec((1, D)), _const_spec((D, wa + 2 * wm)), _const_spec((2 * wa + 2 * wm, D)),
                  _const_spec((16, D)), _const_spec((16, tm)),
                  _const_spec((MLSTM_CONV, 2 * wm)), _const_spec((1, 2 * wm))],
        out_specs=[pl.BlockSpec((None, wa, tm), lambda b, t: (b, 0, t)),
                   btile(L, wa), btile(1, wa), btile(H * vp, L),
                   tile(wm), tile(wm), btile(wm, L), btile(wm, L), btile(16, L)],
        out_shape=[jax.ShapeDtypeStruct((B, wa, S), BF16),
                   bshape(L, wa, BF16), bshape(1, wa, F32), bshape(H * vp, L, BF16),
                   jax.ShapeDtypeStruct((B, S, wm), BF16),
                   jax.ShapeDtypeStruct((B, S, wm), BF16),
                   bshape(wm, L, BF16), bshape(wm, L, BF16), bshape(16, L, F32)],
        scratch_shapes=[pltpu.VMEM((HALO + tm, 2 * wm), F32)],
        compiler_params=params,
    )(x, g1, wn, wt, wgT, bgT, cw1, cb1)

    km = kmean.reshape(B, nb, H, dh)
    km = jnp.pad(km, ((0, 0), (0, 8 - nb), (0, 0), (0, 0)))
    eye = jnp.eye(H, dtype=F32)
    kmt = jnp.einsum('bjhd,hg->bhjgd', km, eye).reshape(B, H * 8, wa)
    kmh = kmt.astype(BF16)
    kml = (kmt - kmh.astype(F32)).astype(BF16)
    ga = att_out_g[0].reshape(wa, 1)

    att = pl.pallas_call(
        _moba_kernel,
        grid=(B, nb),
        in_specs=[pl.BlockSpec((None, wa, L), lambda b, i: (b, 0, i)),
                  pl.BlockSpec((None, nb, L, wa), lambda b, i: (b, 0, 0, 0)),
                  pl.BlockSpec((None, nb, H * vp, L), lambda b, i: (b, 0, 0, 0)),
                  pl.BlockSpec((None, H * 8, wa), lambda b, i: (b, 0, 0)),
                  pl.BlockSpec((None, H * 8, wa), lambda b, i: (b, 0, 0)),
                  _const_spec((wa, 1))],
        out_specs=pl.BlockSpec((None, L, wa), lambda b, i: (b, i, 0)),
        out_shape=jax.ShapeDtypeStruct((B, S, wa), BF16),
        scratch_shapes=[pltpu.VMEM((H * 2 * dh, L), BF16), pltpu.VMEM((H * 8, L), F32), pltpu.VMEM((wa, L), F32),
                        pltpu.VMEM((H * vp, L), F32), pltpu.VMEM((H, L), F32), pltpu.VMEM((2, H, L), F32),
                        pltpu.VMEM((2, H, L), F32), pltpu.VMEM((2, H, L, L), F32)],
        compiler_params=params,
    )(qT, kb, vTb, kmh, kml, ga)

    seq = lambda w: pl.BlockSpec((None, S, w), lambda b: (b, 0, 0))
    blocks = lambda r: pl.BlockSpec((None, nb, r, L), lambda b: (b, 0, 0, 0))
    gn = jnp.broadcast_to(mlstm_out_g[0].reshape(wm, 1), (wm, L))
    mh = pl.pallas_call(
        _mlstm_kernel,
        grid=(B,),
        in_specs=[seq(wm), seq(wm), blocks(wm), blocks(wm), blocks(16), _const_spec((wm, L))],
        out_specs=seq(wm),
        out_shape=jax.ShapeDtypeStruct((B, S, wm), BF16),
        scratch_shapes=[pltpu.VMEM((MLSTM_HEADS, MLSTM_HEAD_DIM + 8, MLSTM_HEAD_DIM), F32),
                        pltpu.VMEM((MLSTM_HEADS, L, L), F32), pltpu.VMEM((wm, L), F32),
                        pltpu.VMEM((nb, 8, L), F32), pltpu.VMEM((nb, 8, L), F32), pltpu.VMEM((nb, L, 128), F32)],
        compiler_params=pltpu.CompilerParams(dimension_semantics=("arbitrary",), vmem_limit_bytes=VMEM_LIMIT),
    )(mq, mk, mvT, moT, gT, gn)

    tf = FFN_ROW_TILE
    assert S % tf == 0
    ftile = lambda w: pl.BlockSpec((None, tf, w), lambda b, t: (b, t, 0))
    out = pl.pallas_call(
        _ffn_kernel,
        grid=(B, S // tf),
        in_specs=[ftile(D), ftile(wa), ftile(wm), _const_spec((wa + wm, D)), _const_spec((1, D)),
                  _const_spec((D, 2 * dff)), _const_spec((FFN_CONV, 2 * dff)), _const_spec((1, 2 * dff)),
                  _const_spec((dff, D)), _const_spec((1, D))],
        out_specs=ftile(D),
        out_shape=jax.ShapeDtypeStruct((B, S, D), x.dtype),
        scratch_shapes=[pltpu.VMEM((HALO, 2 * dff), F32)]
        + [pltpu.VMEM((2, HALO + tf, FFN_COL_CHUNK), F32) for _ in range(FFN_STAGES)],
        compiler_params=params,
    )(x, att, mh, w_out[0].astype(BF16), norm_ffn_g[0].reshape(1, D), w_up[0].astype(BF16), ffn_conv_w[0],
      ffn_conv_b[0].reshape(1, 2 * dff), w_down[0].astype(BF16), norm_final_g.reshape(1, D))
    return out
```

```python
import functools

import jax
import jax.numpy as jnp
from jax import lax
from jax.experimental import pallas as pl
from jax.experimental.pallas import tpu as pltpu

EPS = 1e-6
ATT_HEADS = 8
ATT_HEAD_DIM = 64
MOBA_BLOCK = 256
MOBA_TOPK = 3
MLSTM_HEADS = 4
MLSTM_HEAD_DIM = 128
MLSTM_CONV = 4
FFN_CONV = 3
MLSTM_CHUNK = MOBA_BLOCK
INPROJ_ROW_TILE = 512
INPROJ_CONV_CHUNK = 256
HALO = 8
NEG = -1e30
MOBA_RUN = 8
MOBA_VPAD = 16
LOG2E = 1.4426950408889634
FFN_COL_CHUNK = 256
FFN_ROW_TILE = 512
FFN_STAGES = 3
VMEM_LIMIT = 56 * 1024 * 1024

F32 = jnp.float32
BF16 = jnp.bfloat16

_NT = (((1,), (1,)), ((), ()))
_TN = (((0,), (0,)), ((), ()))


def _sigmoid(y):
    return 1.0 / (1.0 + jnp.exp(-y))


def _rms(x, g):
    return x * lax.rsqrt(jnp.mean(x * x, axis=-1, keepdims=True) + EPS) * g


def _inproj_kernel(x_ref, g_ref, wn_ref, wt_ref, wgT_ref, bgT_ref, cw_ref, cb_ref,
                   qT_ref, k_ref, kmean_ref, vT_ref, mq_ref, mk_ref, mvT_ref, moT_ref, gT_ref,
                   conv_ref):
    t = pl.program_id(1)
    tm = x_ref.shape[0]
    L = MOBA_BLOCK
    wa = k_ref.shape[2]
    wm = mq_ref.shape[1]

    @pl.when(t == 0)
    def _():
        conv_ref[0:HALO, :] = jnp.zeros((HALO, 2 * wm), F32)

    a = _rms(x_ref[...], g_ref[...]).astype(BF16)
    cc = INPROJ_CONV_CHUNK

    def project_conv(j):
        cols = slice(j * cc, (j + 1) * cc)
        conv_ref[HALO:HALO + tm, cols] = jnp.dot(a, wn_ref[:, wa + j * cc:wa + (j + 1) * cc],
                                                 preferred_element_type=F32)

    def finish_conv(j):
        cols = slice(j * cc, (j + 1) * cc)
        y = cb_ref[:, cols]
        for tap in range(MLSTM_CONV):
            off = HALO - (MLSTM_CONV - 1) + tap
            y = y + cw_ref[tap:tap + 1, cols] * conv_ref[off:off + tm, cols]
        conv_ref[0:HALO, cols] = conv_ref[tm:tm + HALO, cols]
        y = y.astype(BF16)
        y = y * (1.0 / (1.0 + jnp.exp2(y * (-LOG2E))))
        if (j + 1) * cc <= wm:
            mq_ref[:, cols] = y * (MLSTM_HEAD_DIM ** -0.5)
        else:
            mk_ref[:, j * cc - wm:(j + 1) * cc - wm] = y

    def project_k():
        k = jnp.dot(a, wn_ref[:, 0:wa], preferred_element_type=F32)
        for i in range(tm // L):
            blk = k[i * L:(i + 1) * L, :]
            k_ref[i] = blk.astype(BF16)
            kmean_ref[i] = jnp.mean(blk, axis=0, keepdims=True)

    def transposed(lo, hi):
        return lax.dot_general(wt_ref[lo:hi, :], a, _NT, preferred_element_type=F32)

    def project_qT():
        vals = transposed(0, wa) * (ATT_HEAD_DIM ** -0.5 * LOG2E)
        for i in range(tm // L):
            qT_ref[i] = vals[:, i * L:(i + 1) * L].astype(BF16)

    def project_blocks(ref, lo, hi):
        vals = transposed(lo, hi)
        for i in range(tm // L):
            ref[i] = vals[:, i * L:(i + 1) * L].astype(BF16)

    def project_vT():
        vals = transposed(wa, 2 * wa)
        dh = ATT_HEAD_DIM
        vp = dh + MOBA_VPAD
        ones_row = (lax.broadcasted_iota(jnp.int32, (MOBA_VPAD, L), 0) == 0).astype(BF16)
        for i in range(tm // L):
            for h in range(ATT_HEADS):
                vT_ref[i, h * vp:h * vp + dh, :] = vals[h * dh:(h + 1) * dh, i * L:(i + 1) * L].astype(BF16)
                vT_ref[i, h * vp + dh:(h + 1) * vp, :] = ones_row

    def project_gates():
        gT = lax.dot_general(wgT_ref[...], a, _NT, preferred_element_type=F32) + bgT_ref[...]
        for i in range(tm // L):
            gT_ref[i] = gT[:, i * L:(i + 1) * L]

    others = [project_k, project_qT, project_vT,
              lambda: project_blocks(mvT_ref, 2 * wa, 2 * wa + wm),
              lambda: project_blocks(moT_ref, 2 * wa + wm, 2 * wa + 2 * wm), project_gates]
    nconv = 2 * wm // cc
    project_conv(0)
    for j in range(nconv):
        if j + 1 < nconv:
            project_conv(j + 1)
        if others:
            others.pop(0)()
        finish_conv(j)
    for f in others:
        f()


def _moba_block(i, qT_ref, k_ref, vT_ref, kmh_ref, kml_ref, g_ref, o_ref, qz_ref, bias_ref, oT_ref, acc_ref, m_ref,
                alpha_ref, sub_ref, s_ref):
    L = MOBA_BLOCK
    dh = ATT_HEAD_DIM
    nbp = 8
    def prepare():
        qT = qT_ref[i]
        zeros = jnp.zeros((dh, L), BF16)
        for h in range(ATT_HEADS):
            qh = qT[h * dh:(h + 1) * dh, :]
            lo, hi = (qh, zeros) if h % 2 == 0 else (zeros, qh)
            qz_ref[h * 2 * dh:h * 2 * dh + dh, :] = lo
            qz_ref[h * 2 * dh + dh:(h + 1) * 2 * dh, :] = hi
        gate = (jnp.dot(kmh_ref[...], qT, preferred_element_type=F32)
                + jnp.dot(kml_ref[...], qT, preferred_element_type=F32))
        jrow = lax.broadcasted_iota(jnp.int32, (nbp, L), 0)
        past = jrow < i
        for h in range(ATT_HEADS):
            g = gate[h * nbp:(h + 1) * nbp, :]
            gm = jnp.where(past, g, -jnp.inf)
            rank = jnp.zeros((nbp, L), jnp.int32)
            for ii in range(nbp):
                gi = gm[ii:ii + 1, :]
                beats = jnp.where(jrow > ii, jnp.where(gi >= g, 1, 0), jnp.where(gi > g, 1, 0))
                rank = rank + beats
            sel = past & (rank < MOBA_TOPK)
            bias_ref[h * nbp:(h + 1) * nbp, :] = jnp.where(sel, 0.0, NEG)

    krow = lax.broadcasted_iota(jnp.int32, (L, L), 0)
    qcol = lax.broadcasted_iota(jnp.int32, (L, L), 1)
    causal = krow <= qcol

    vp = dh + MOBA_VPAD

    def scores(j, slot, own):
        for p in range(ATT_HEADS // 2):
            k_j = k_ref[j, :, p * 2 * dh:(p + 1) * 2 * dh]
            for h in (2 * p, 2 * p + 1):
                s = jnp.dot(k_j, qz_ref[h * 2 * dh:(h + 1) * 2 * dh, :], preferred_element_type=F32)
                if own:
                    s = jnp.where(causal, s, NEG)
                    m_new = jnp.max(s, axis=0, keepdims=True)
                    alpha = jnp.zeros((1, L), F32)
                    sub = m_new
                else:
                    b = bias_ref[pl.ds(h * nbp + j, 1), :]
                    m = m_ref[h:h + 1, :]
                    m_new = jnp.maximum(m, jnp.max(s, axis=0, keepdims=True) + b)
                    alpha = jnp.exp2(m - m_new)
                    sub = m_new - b
                s_ref[slot, h] = s
                m_ref[h:h + 1, :] = m_new
                alpha_ref[slot, pl.ds(h, 1), :] = alpha
                sub_ref[slot, pl.ds(h, 1), :] = sub

    def values(j, slot):
        for h in range(ATT_HEADS):
            rows = slice(h * vp, (h + 1) * vp)
            pr = jnp.exp2(s_ref[slot, h] - sub_ref[slot, pl.ds(h, 1), :])
            pv = jnp.dot(vT_ref[j, rows, :], pr.astype(BF16), preferred_element_type=F32)
            acc_ref[rows, :] = alpha_ref[slot, pl.ds(h, 1), :] * acc_ref[rows, :] + pv

    def visit_run(blocks):
        scores(blocks[0][0], 0, blocks[0][1])
        for r, (j, _) in enumerate(blocks):
            if r + 1 < len(blocks):
                scores(blocks[r + 1][0], (r + 1) % 2, blocks[r + 1][1])
            values(j, r % 2)

    acc_ref[...] = jnp.zeros(acc_ref.shape, F32)
    tail = lax.rem(i, MOBA_RUN)
    for n in range(MOBA_RUN):
        @pl.when(tail == n)
        def _(n=n):
            prepare()
            visit_run([(i, True)] + [(r, False) for r in range(n)])

    def full_run(t, carry):
        visit_run([(tail + MOBA_RUN * t + r, False) for r in range(MOBA_RUN)])
        return carry

    lax.fori_loop(0, i // MOBA_RUN, full_run, 0)

    for h in range(ATT_HEADS):
        rows = slice(h * dh, (h + 1) * dh)
        acc = acc_ref[h * vp:(h + 1) * vp, :]
        o = acc[0:dh, :] * (1.0 / acc[dh:dh + 1, :])
        oT_ref[rows, :] = o * lax.rsqrt(jnp.mean(o * o, axis=0, keepdims=True) + EPS) * g_ref[rows, :]

    o_ref[pl.ds(pl.multiple_of(i * L, L), L), :] = oT_ref[...].T.astype(BF16)


def _moba_kernel(*refs):
    def qblock(i, carry):
        _moba_block(i, *refs)
        return carry

    lax.fori_loop(0, refs[1].shape[0], qblock, 0)


def _split3(x):
    hi = x.astype(BF16)
    r = x - hi.astype(F32)
    mid = r.astype(BF16)
    lo = (r - mid.astype(F32)).astype(BF16)
    return hi, mid, lo


def _mlstm_kernel(q_ref, k_ref, vT_ref, oT_ref, gT_ref, gn_ref, out_ref, c_ref, s_ref, hT_ref, b_ref, crow_ref,
                  ccol_ref):
    nc, _, L = vT_ref.shape
    D = MLSTM_HEAD_DIM
    H = MLSTM_HEADS
    c_ref[...] = jnp.zeros(c_ref.shape, F32)
    srow = lax.broadcasted_iota(jnp.int32, (L, L), 0)
    tcol = lax.broadcasted_iota(jnp.int32, (L, L), 1)
    causal = srow <= tcol
    upper = causal.astype(BF16)
    first_row = lax.broadcasted_iota(jnp.int32, (8, L), 0) == 0

    for c in range(nc):
        gT = gT_ref[c]
        lfT = (jnp.minimum(gT, 0.0) - jnp.log(1.0 + jnp.exp(-jnp.abs(gT)))) * LOG2E
        bT = sum(jnp.dot(part, upper, preferred_element_type=F32) for part in _split3(lfT))
        c8 = gT[0:8, :] * LOG2E - bT[8:16, :]
        b_ref[c] = bT[8:16, :]
        crow_ref[c] = c8
        ccol_ref[c] = jnp.concatenate([c8, jnp.zeros((128 - 8, L), F32)], axis=0).T

    def chunk(c, ms):
        rows = pl.ds(pl.multiple_of(c * L, L), L)
        b8 = b_ref[c]
        c8 = crow_ref[c]
        c_cols = ccol_ref[c]
        inters = []
        for h in range(H):
            cols = slice(h * D, (h + 1) * D)
            q = q_ref[rows, cols]
            s_ref[h] = lax.dot_general(k_ref[rows, cols], q, _NT, preferred_element_type=F32)
            inters.append(lax.dot_general(c_ref[h].astype(BF16), q, _NT, preferred_element_type=F32))
        new_ms = []
        for h in range(H):
            cols = slice(h * D, (h + 1) * D)
            m_prev = ms[h]
            b_row = b8[h:h + 1, :]
            c_row = c8[h:h + 1, :]
            cm = jnp.where(causal, c_cols[:, h:h + 1], NEG)
            mx = jnp.maximum(m_prev, jnp.max(cm, axis=0, keepdims=True))
            w_inter = jnp.exp2(m_prev - mx)
            sT = s_ref[h] * jnp.exp2(cm - mx)
            vT = vT_ref[c, cols, :]
            inter = inters[h]
            num = w_inter * inter[0:D, :] + jnp.dot(vT, sT.astype(BF16), preferred_element_type=F32)
            den = w_inter * inter[D:D + 1, :] + jnp.sum(sT, axis=0, keepdims=True)
            m_t = b_row + mx
            hid = num * (1.0 / jnp.maximum(jnp.abs(den), jnp.exp2(-m_t)))
            hid = hid * lax.rsqrt(jnp.mean(hid * hid, axis=0, keepdims=True) + EPS) * gn_ref[cols, :]
            hT_ref[cols, :] = hid * _sigmoid(oT_ref[c, cols, :].astype(F32))

            m_last = m_t[:, L - 1:L]
            decay = w_inter[:, L - 1:L]
            w_last = jnp.exp2(c_row + (b_row[:, L - 1:L] - m_last))
            lhs = jnp.concatenate([vT.astype(F32) * w_last,
                                   jnp.where(first_row, w_last, 0.0)], axis=0).astype(BF16)
            c_ref[h] = decay * c_ref[h] + jnp.dot(lhs, k_ref[rows, cols], preferred_element_type=F32)
            new_ms.append(m_last)
        out_ref[rows, :] = hT_ref[...].T.astype(out_ref.dtype)
        return tuple(new_ms)

    lax.fori_loop(0, nc, chunk, tuple(jnp.zeros((1, 1), F32) for _ in range(H)))


def _ffn_kernel(x_ref, att_ref, mh_ref, wout_ref, g2_ref, wup_ref, cw_ref, cb_ref, wdown_ref, g3_ref,
                out_ref, halo_ref, *stage_refs):
    t = pl.program_id(1)
    tm = x_ref.shape[0]
    dff = wdown_ref.shape[0]
    ch = FFN_COL_CHUNK
    nst = len(stage_refs)

    heads = jnp.concatenate([att_ref[...], mh_ref[...]], axis=1)
    h1 = x_ref[...] + jnp.dot(heads, wout_ref[...], preferred_element_type=F32)
    a2 = _rms(h1, g2_ref[...]).astype(BF16)

    @pl.when(t == 0)
    def _():
        halo_ref[...] = jnp.zeros(halo_ref.shape, F32)

    def project(c):
        st = stage_refs[c % nst]
        for i, off in enumerate((c * ch, dff + c * ch)):
            st[i, HALO:HALO + tm, :] = jnp.dot(a2, wup_ref[:, off:off + ch], preferred_element_type=F32)

    def gate(c):
        st = stage_refs[c % nst]
        ys = []
        for i, off in enumerate((c * ch, dff + c * ch)):
            cols = slice(off, off + ch)
            st[i, 0:HALO, :] = halo_ref[:, cols]
            y = cb_ref[:, cols]
            for j in range(FFN_CONV):
                o = HALO - (FFN_CONV - 1) + j
                y = y + cw_ref[j:j + 1, cols] * st[i, o:o + tm, :]
            halo_ref[:, cols] = st[i, tm:tm + HALO, :]
            ys.append(y.astype(BF16))
        sig = 1.0 / (1.0 + jnp.exp2(ys[0] * (-LOG2E)))
        return ys[0] * sig * ys[1]

    nch = dff // ch
    out_ref[...] = h1
    project(0)
    project(1)
    act = gate(0)
    for c in range(nch):
        if c + 2 < nch:
            project(c + 2)
        act_next = gate(c + 1) if c + 1 < nch else None
        out_ref[...] += jnp.dot(act, wdown_ref[c * ch:(c + 1) * ch, :], preferred_element_type=F32)
        act = act_next
    out_ref[...] = _rms(out_ref[...], g3_ref[...])


def _const_spec(shape):
    return pl.BlockSpec(shape, lambda *_: (0,) * len(shape))


def kernel(x, norm_mix_g, w_in, b_gates, mlstm_conv_w, mlstm_conv_b, att_out_g, mlstm_out_g, w_out, norm_ffn_g, w_up, ffn_conv_w, ffn_conv_b, w_down, norm_final_g):
    B, S, D = x.shape
    H, dh = ATT_HEADS, ATT_HEAD_DIM
    wa = H * dh
    vp = dh + MOBA_VPAD
    wm = MLSTM_HEADS * MLSTM_HEAD_DIM
    L = MOBA_BLOCK
    nb = S // L
    assert S % L == 0 and nb <= 8 and MLSTM_CHUNK == L
    assert w_in.shape[0] == 1 and w_in.shape[2] == 3 * wa + 4 * wm + 2 * MLSTM_HEADS
    dff = w_down.shape[1]
    assert dff % FFN_COL_CHUNK == 0

    wi = w_in[0]
    o_mq, o_mv, o_g = 3 * wa, 3 * wa + 2 * wm, 3 * wa + 4 * wm
    wn = jnp.concatenate([wi[:, wa:2 * wa], wi[:, o_mq:o_mv]], axis=1).astype(BF16)
    wt = jnp.concatenate([wi[:, 0:wa], wi[:, 2 * wa:3 * wa], wi[:, o_mv:o_g]], axis=1).T.astype(BF16)
    nh = MLSTM_HEADS
    wgT = jnp.zeros((16, D), F32).at[0:nh].set(wi[:, o_g:o_g + nh].T).at[8:8 + nh].set(wi[:, o_g + nh:].T).astype(BF16)
    bgT = jnp.zeros((16,), F32).at[0:nh].set(b_gates[0, 0:nh]).at[8:8 + nh].set(b_gates[0, nh:])
    tm = INPROJ_ROW_TILE
    assert S % tm == 0 and tm % L == 0
    bgT = jnp.broadcast_to(bgT.reshape(16, 1), (16, tm))
    g1 = norm_mix_g[0].reshape(1, D)
    cw1 = mlstm_conv_w[0]
    cb1 = mlstm_conv_b[0].reshape(1, 2 * wm)

    nbt = tm // L
    params = pltpu.CompilerParams(dimension_semantics=("arbitrary", "arbitrary"), vmem_limit_bytes=VMEM_LIMIT)
    tile = lambda w: pl.BlockSpec((None, tm, w), lambda b, t: (b, t, 0))
    btile = lambda r, c: pl.BlockSpec((None, nbt, r, c), lambda b, t: (b, t, 0, 0))
    bshape = lambda r, c, dt: jax.ShapeDtypeStruct((B, nb, r, c), dt)
    qT, kb, kmean, vTb, mq, mk, mvT, moT, gT = pl.pallas_call(
        _inproj_kernel,
        grid=(B, S // tm),
        in_specs=[tile(D), _const_spec((1, D)), _const_spec((D, wa + 2 * wm)), _const_spec((2 * wa + 2 * wm, D)),
                  _const_spec((16, D)), _const_spec((16, tm)),
                  _const_spec((MLSTM_CONV, 2 * wm)), _const_spec((1, 2 * wm))],
        out_specs=[btile(wa, L), btile(L, wa), btile(1, wa), btile(H * vp, L),
                   tile(wm), tile(wm), btile(wm, L), btile(wm, L), btile(16, L)],
        out_shape=[bshape(wa, L, BF16), bshape(L, wa, BF16), bshape(1, wa, F32), bshape(H * vp, L, BF16),
                   jax.ShapeDtypeStruct((B, S, wm), BF16),
                   jax.ShapeDtypeStruct((B, S, wm), BF16),
                   bshape(wm, L, BF16), bshape(wm, L, BF16), bshape(16, L, F32)],
        scratch_shapes=[pltpu.VMEM((HALO + tm, 2 * wm), F32)],
        compiler_params=params,
    )(x, g1, wn, wt, wgT, bgT, cw1, cb1)

    km = kmean.reshape(B, nb, H, dh)
    km = jnp.pad(km, ((0, 0), (0, 8 - nb), (0, 0), (0, 0)))
    eye = jnp.eye(H, dtype=F32)
    kmt = jnp.einsum('bjhd,hg->bhjgd', km, eye).reshape(B, H * 8, wa)
    kmh = kmt.astype(BF16)
    kml = (kmt - kmh.astype(F32)).astype(BF16)
    ga = att_out_g[0].reshape(wa, 1)

    att = pl.pallas_call(
        _moba_kernel,
        grid=(B,),
        in_specs=[pl.BlockSpec((None, nb, wa, L), lambda b: (b, 0, 0, 0)),
                  pl.BlockSpec((None, nb, L, wa), lambda b: (b, 0, 0, 0)),
                  pl.BlockSpec((None, nb, H * vp, L), lambda b: (b, 0, 0, 0)),
                  pl.BlockSpec((None, H * 8, wa), lambda b: (b, 0, 0)),
                  pl.BlockSpec((None, H * 8, wa), lambda b: (b, 0, 0)),
                  _const_spec((wa, 1))],
        out_specs=pl.BlockSpec((None, S, wa), lambda b: (b, 0, 0)),
        out_shape=jax.ShapeDtypeStruct((B, S, wa), BF16),
        scratch_shapes=[pltpu.VMEM((H * 2 * dh, L), BF16), pltpu.VMEM((H * 8, L), F32), pltpu.VMEM((wa, L), F32),
                        pltpu.VMEM((H * vp, L), F32), pltpu.VMEM((H, L), F32), pltpu.VMEM((2, H, L), F32),
                        pltpu.VMEM((2, H, L), F32), pltpu.VMEM((2, H, L, L), F32)],
        compiler_params=pltpu.CompilerParams(dimension_semantics=("arbitrary",), vmem_limit_bytes=VMEM_LIMIT),
    )(qT, kb, vTb, kmh, kml, ga)

    seq = lambda w: pl.BlockSpec((None, S, w), lambda b: (b, 0, 0))
    blocks = lambda r: pl.BlockSpec((None, nb, r, L), lambda b: (b, 0, 0, 0))
    gn = jnp.broadcast_to(mlstm_out_g[0].reshape(wm, 1), (wm, L))
    mh = pl.pallas_call(
        _mlstm_kernel,
        grid=(B,),
        in_specs=[seq(wm), seq(wm), blocks(wm), blocks(wm), blocks(16), _const_spec((wm, L))],
        out_specs=seq(wm),
        out_shape=jax.ShapeDtypeStruct((B, S, wm), BF16),
        scratch_shapes=[pltpu.VMEM((MLSTM_HEADS, MLSTM_HEAD_DIM + 8, MLSTM_HEAD_DIM), F32),
                        pltpu.VMEM((MLSTM_HEADS, L, L), F32), pltpu.VMEM((wm, L), F32),
                        pltpu.VMEM((nb, 8, L), F32), pltpu.VMEM((nb, 8, L), F32), pltpu.VMEM((nb, L, 128), F32)],
        compiler_params=pltpu.CompilerParams(dimension_semantics=("arbitrary",), vmem_limit_bytes=VMEM_LIMIT),
    )(mq, mk, mvT, moT, gT, gn)

    tf = FFN_ROW_TILE
    assert S % tf == 0
    ftile = lambda w: pl.BlockSpec((None, tf, w), lambda b, t: (b, t, 0))
    out = pl.pallas_call(
        _ffn_kernel,
        grid=(B, S // tf),
        in_specs=[ftile(D), ftile(wa), ftile(wm), _const_spec((wa + wm, D)), _const_spec((1, D)),
                  _const_spec((D, 2 * dff)), _const_spec((FFN_CONV, 2 * dff)), _const_spec((1, 2 * dff)),
                  _const_spec((dff, D)), _const_spec((1, D))],
        out_specs=ftile(D),
        out_shape=jax.ShapeDtypeStruct((B, S, D), x.dtype),
        scratch_shapes=[pltpu.VMEM((HALO, 2 * dff), F32)]
        + [pltpu.VMEM((2, HALO + tf, FFN_COL_CHUNK), F32) for _ in range(FFN_STAGES)],
        compiler_params=params,
    )(x, att, mh, w_out[0].astype(BF16), norm_ffn_g[0].reshape(1, D), w_up[0].astype(BF16), ffn_conv_w[0],
      ffn_conv_b[0].reshape(1, 2 * dff), w_down[0].astype(BF16), norm_final_g.reshape(1, D))
    return out
```

```python
import jax
import jax.numpy as jnp
from jax import lax
from jax.experimental import pallas as pl
from jax.experimental.pallas import tpu as pltpu

EPS = 1e-6
ATT_HEADS = 8
ATT_HEAD_DIM = 64
MOBA_BLOCK = 256
MOBA_TOPK = 3
MLSTM_HEADS = 4
MLSTM_HEAD_DIM = 128
MLSTM_CONV = 4
FFN_CONV = 3
MLSTM_CHUNK = MOBA_BLOCK
INPROJ_ROW_TILE = 512
INPROJ_CONV_CHUNK = 256
HALO = 8
NEG = -1e30
MOBA_RUN = 8
MOBA_VPAD = 16
LOG2E = 1.4426950408889634
FFN_COL_CHUNK = 256
FFN_ROW_TILE = 512
FFN_STAGES = 3
VMEM_LIMIT = 56 * 1024 * 1024

F32 = jnp.float32
BF16 = jnp.bfloat16

_NT = (((1,), (1,)), ((), ()))


def _sigmoid(y):
    return 1.0 / (1.0 + jnp.exp(-y))


def _rms(x, g):
    return x * lax.rsqrt(jnp.mean(x * x, axis=-1, keepdims=True) + EPS) * g


def _inproj_kernel(x_ref, g_ref, wn_ref, wt_ref, wgT_ref, bgT_ref, cw_ref, cb_ref,
                   qT_ref, k_ref, kmean_ref, vT_ref, mq_ref, mk_ref, mvT_ref, moT_ref, gT_ref,
                   conv_ref):
    t = pl.program_id(1)
    tm = x_ref.shape[0]
    L = MOBA_BLOCK
    wa = k_ref.shape[2]
    wm = mq_ref.shape[1]

    @pl.when(t == 0)
    def _():
        conv_ref[0:HALO, :] = jnp.zeros((HALO, 2 * wm), F32)

    a = _rms(x_ref[...], g_ref[...]).astype(BF16)
    cc = INPROJ_CONV_CHUNK

    def project_conv(j):
        cols = slice(j * cc, (j + 1) * cc)
        conv_ref[HALO:HALO + tm, cols] = jnp.dot(a, wn_ref[:, wa + j * cc:wa + (j + 1) * cc],
                                                 preferred_element_type=F32)

    def finish_conv(j):
        cols = slice(j * cc, (j + 1) * cc)
        y = cb_ref[:, cols]
        for tap in range(MLSTM_CONV):
            off = HALO - (MLSTM_CONV - 1) + tap
            y = y + cw_ref[tap:tap + 1, cols] * conv_ref[off:off + tm, cols]
        conv_ref[0:HALO, cols] = conv_ref[tm:tm + HALO, cols]
        y = y.astype(BF16)
        y = y * (1.0 / (1.0 + jnp.exp2(y * (-LOG2E))))
        if (j + 1) * cc <= wm:
            mq_ref[:, cols] = y * (MLSTM_HEAD_DIM ** -0.5)
        else:
            mk_ref[:, j * cc - wm:(j + 1) * cc - wm] = y

    def project_k():
        k = jnp.dot(a, wn_ref[:, 0:wa], preferred_element_type=F32)
        for i in range(tm // L):
            blk = k[i * L:(i + 1) * L, :]
            k_ref[i] = blk.astype(BF16)
            kmean_ref[i] = jnp.mean(blk, axis=0, keepdims=True)

    def transposed(lo, hi):
        return lax.dot_general(wt_ref[lo:hi, :], a, _NT, preferred_element_type=F32)

    def project_qT():
        vals = transposed(0, wa) * (ATT_HEAD_DIM ** -0.5 * LOG2E)
        for i in range(tm // L):
            qT_ref[i] = vals[:, i * L:(i + 1) * L].astype(BF16)

    def project_blocks(ref, lo, hi):
        vals = transposed(lo, hi)
        for i in range(tm // L):
            ref[i] = vals[:, i * L:(i + 1) * L].astype(BF16)

    def project_vT():
        vals = transposed(wa, 2 * wa)
        dh = ATT_HEAD_DIM
        vp = dh + MOBA_VPAD
        ones_row = (lax.broadcasted_iota(jnp.int32, (MOBA_VPAD, L), 0) == 0).astype(BF16)
        for i in range(tm // L):
            for h in range(ATT_HEADS):
                vT_ref[i, h * vp:h * vp + dh, :] = vals[h * dh:(h + 1) * dh, i * L:(i + 1) * L].astype(BF16)
                vT_ref[i, h * vp + dh:(h + 1) * vp, :] = ones_row

    def project_gates():
        gT = lax.dot_general(wgT_ref[...], a, _NT, preferred_element_type=F32) + bgT_ref[...]
        for i in range(tm // L):
            gT_ref[i] = gT[:, i * L:(i + 1) * L]

    others = [project_k, project_qT, project_vT,
              lambda: project_blocks(mvT_ref, 2 * wa, 2 * wa + wm),
              lambda: project_blocks(moT_ref, 2 * wa + wm, 2 * wa + 2 * wm), project_gates]
    nconv = 2 * wm // cc
    project_conv(0)
    for j in range(nconv):
        if j + 1 < nconv:
            project_conv(j + 1)
        if others:
            others.pop(0)()
        finish_conv(j)
    for f in others:
        f()


def _moba_block(i, qT_ref, k_ref, vT_ref, kmh_ref, kml_ref, g_ref, o_ref, qz_ref, bias_ref, oT_ref, acc_ref, m_ref,
                alpha_ref, sub_ref, s_ref):
    L = MOBA_BLOCK
    dh = ATT_HEAD_DIM
    nbp = 8
    def prepare():
        qT = qT_ref[i]
        zeros = jnp.zeros((dh, L), BF16)
        for h in range(ATT_HEADS):
            qh = qT[h * dh:(h + 1) * dh, :]
            lo, hi = (qh, zeros) if h % 2 == 0 else (zeros, qh)
            qz_ref[h * 2 * dh:h * 2 * dh + dh, :] = lo
            qz_ref[h * 2 * dh + dh:(h + 1) * 2 * dh, :] = hi
        gate = (jnp.dot(kmh_ref[...], qT, preferred_element_type=F32)
                + jnp.dot(kml_ref[...], qT, preferred_element_type=F32))
        jrow = lax.broadcasted_iota(jnp.int32, (nbp, L), 0)
        past = jrow < i
        for h in range(ATT_HEADS):
            g = gate[h * nbp:(h + 1) * nbp, :]
            gm = jnp.where(past, g, -jnp.inf)
            rank = jnp.zeros((nbp, L), jnp.int32)
            for ii in range(nbp):
                gi = gm[ii:ii + 1, :]
                beats = jnp.where(jrow > ii, jnp.where(gi >= g, 1, 0), jnp.where(gi > g, 1, 0))
                rank = rank + beats
            sel = past & (rank < MOBA_TOPK)
            bias_ref[h * nbp:(h + 1) * nbp, :] = jnp.where(sel, 0.0, NEG)

    krow = lax.broadcasted_iota(jnp.int32, (L, L), 0)
    qcol = lax.broadcasted_iota(jnp.int32, (L, L), 1)
    causal = krow <= qcol

    vp = dh + MOBA_VPAD

    def scores(j, slot, own):
        for p in range(ATT_HEADS // 2):
            k_j = k_ref[j, :, p * 2 * dh:(p + 1) * 2 * dh]
            for h in (2 * p, 2 * p + 1):
                s = jnp.dot(k_j, qz_ref[h * 2 * dh:(h + 1) * 2 * dh, :], preferred_element_type=F32)
                if own:
                    s = jnp.where(causal, s, NEG)
                    m_new = jnp.max(s, axis=0, keepdims=True)
                    alpha = jnp.zeros((1, L), F32)
                    sub = m_new
                else:
                    b = bias_ref[pl.ds(h * nbp + j, 1), :]
                    m = m_ref[h:h + 1, :]
                    m_new = jnp.maximum(m, jnp.max(s, axis=0, keepdims=True) + b)
                    alpha = jnp.exp2(m - m_new)
                    sub = m_new - b
                s_ref[slot, h] = s
                m_ref[h:h + 1, :] = m_new
                alpha_ref[slot, pl.ds(h, 1), :] = alpha
                sub_ref[slot, pl.ds(h, 1), :] = sub

    def values(j, slot):
        for h in range(ATT_HEADS):
            rows = slice(h * vp, (h + 1) * vp)
            pr = jnp.exp2(s_ref[slot, h] - sub_ref[slot, pl.ds(h, 1), :])
            pv = jnp.dot(vT_ref[j, rows, :], pr.astype(BF16), preferred_element_type=F32)
            acc_ref[rows, :] = alpha_ref[slot, pl.ds(h, 1), :] * acc_ref[rows, :] + pv

    def visit_run(blocks):
        scores(blocks[0][0], 0, blocks[0][1])
        for r, (j, _) in enumerate(blocks):
            if r + 1 < len(blocks):
                scores(blocks[r + 1][0], (r + 1) % 2, blocks[r + 1][1])
            values(j, r % 2)

    acc_ref[...] = jnp.zeros(acc_ref.shape, F32)
    tail = lax.rem(i, MOBA_RUN)
    for n in range(MOBA_RUN):
        @pl.when(tail == n)
        def _(n=n):
            prepare()
            visit_run([(i, True)] + [(r, False) for r in range(n)])

    def full_run(t, carry):
        visit_run([(tail + MOBA_RUN * t + r, False) for r in range(MOBA_RUN)])
        return carry

    lax.fori_loop(0, i // MOBA_RUN, full_run, 0)

    for h in range(ATT_HEADS):
        rows = slice(h * dh, (h + 1) * dh)
        acc = acc_ref[h * vp:(h + 1) * vp, :]
        o = acc[0:dh, :] * (1.0 / acc[dh:dh + 1, :])
        oT_ref[rows, :] = o * lax.rsqrt(jnp.mean(o * o, axis=0, keepdims=True) + EPS) * g_ref[rows, :]

    o_ref[pl.ds(pl.multiple_of(i * L, L), L), :] = oT_ref[...].T.astype(BF16)


def _moba_kernel(*refs):
    def qblock(i, carry):
        _moba_block(i, *refs)
        return carry

    lax.fori_loop(0, refs[1].shape[0], qblock, 0)


def _split3(x):
    hi = x.astype(BF16)
    r = x - hi.astype(F32)
    mid = r.astype(BF16)
    lo = (r - mid.astype(F32)).astype(BF16)
    return hi, mid, lo


def _mlstm_kernel(q_ref, k_ref, vT_ref, oT_ref, gT_ref, gn_ref, out_ref, c_ref, s_ref, hT_ref, b_ref, crow_ref,
                  ccol_ref):
    nc, _, L = vT_ref.shape
    D = MLSTM_HEAD_DIM
    H = MLSTM_HEADS
    c_ref[...] = jnp.zeros(c_ref.shape, F32)
    srow = lax.broadcasted_iota(jnp.int32, (L, L), 0)
    tcol = lax.broadcasted_iota(jnp.int32, (L, L), 1)
    causal = srow <= tcol
    upper = causal.astype(BF16)
    first_row = lax.broadcasted_iota(jnp.int32, (8, L), 0) == 0

    for c in range(nc):
        gT = gT_ref[c]
        lfT = (jnp.minimum(gT, 0.0) - jnp.log(1.0 + jnp.exp(-jnp.abs(gT)))) * LOG2E
        bT = sum(jnp.dot(part, upper, preferred_element_type=F32) for part in _split3(lfT))
        c8 = gT[0:8, :] * LOG2E - bT[8:16, :]
        b_ref[c] = bT[8:16, :]
        crow_ref[c] = c8
        ccol_ref[c] = jnp.concatenate([c8, jnp.zeros((128 - 8, L), F32)], axis=0).T

    def chunk(c, ms):
        rows = pl.ds(pl.multiple_of(c * L, L), L)
        b8 = b_ref[c]
        c8 = crow_ref[c]
        c_cols = ccol_ref[c]
        inters = []
        for h in range(H):
            cols = slice(h * D, (h + 1) * D)
            q = q_ref[rows, cols]
            s_ref[h] = lax.dot_general(k_ref[rows, cols], q, _NT, preferred_element_type=F32)
            inters.append(lax.dot_general(c_ref[h].astype(BF16), q, _NT, preferred_element_type=F32))
        new_ms = []
        for h in range(H):
            cols = slice(h * D, (h + 1) * D)
            m_prev = ms[h]
            b_row = b8[h:h + 1, :]
            c_row = c8[h:h + 1, :]
            cm = jnp.where(causal, c_cols[:, h:h + 1], NEG)
            mx = jnp.maximum(m_prev, jnp.max(cm, axis=0, keepdims=True))
            w_inter = jnp.exp2(m_prev - mx)
            sT = s_ref[h] * jnp.exp2(cm - mx)
            vT = vT_ref[c, cols, :]
            inter = inters[h]
            num = w_inter * inter[0:D, :] + jnp.dot(vT, sT.astype(BF16), preferred_element_type=F32)
            den = w_inter * inter[D:D + 1, :] + jnp.sum(sT, axis=0, keepdims=True)
            m_t = b_row + mx
            hid = num * (1.0 / jnp.maximum(jnp.abs(den), jnp.exp2(-m_t)))
            hid = hid * lax.rsqrt(jnp.mean(hid * hid, axis=0, keepdims=True) + EPS) * gn_ref[cols, :]
            hT_ref[cols, :] = hid * _sigmoid(oT_ref[c, cols, :].astype(F32))

            m_last = m_t[:, L - 1:L]
            decay = w_inter[:, L - 1:L]
            w_last = jnp.exp2(c_row + (b_row[:, L - 1:L] - m_last))
            lhs = jnp.concatenate([vT.astype(F32) * w_last,
                                   jnp.where(first_row, w_last, 0.0)], axis=0).astype(BF16)
            c_ref[h] = decay * c_ref[h] + jnp.dot(lhs, k_ref[rows, cols], preferred_element_type=F32)
            new_ms.append(m_last)
        out_ref[rows, :] = hT_ref[...].T.astype(out_ref.dtype)
        return tuple(new_ms)

    lax.fori_loop(0, nc, chunk, tuple(jnp.zeros((1, 1), F32) for _ in range(H)))


def _ffn_kernel(x_ref, att_ref, mh_ref, wout_ref, g2_ref, wup_ref, cw_ref, cb_ref, wdown_ref, g3_ref,
                out_ref, halo_ref, *stage_refs):
    t = pl.program_id(1)
    tm = x_ref.shape[0]
    dff = wdown_ref.shape[0]
    ch = FFN_COL_CHUNK
    nst = len(stage_refs)

    heads = jnp.concatenate([att_ref[...], mh_ref[...]], axis=1)
    h1 = x_ref[...] + jnp.dot(heads, wout_ref[...], preferred_element_type=F32)
    a2 = _rms(h1, g2_ref[...]).astype(BF16)

    @pl.when(t == 0)
    def _():
        halo_ref[...] = jnp.zeros(halo_ref.shape, F32)

    def project(c):
        st = stage_refs[c % nst]
        for i, off in enumerate((c * ch, dff + c * ch)):
            st[i, HALO:HALO + tm, :] = jnp.dot(a2, wup_ref[:, off:off + ch], preferred_element_type=F32)

    def gate(c):
        st = stage_refs[c % nst]
        ys = []
        for i, off in enumerate((c * ch, dff + c * ch)):
            cols = slice(off, off + ch)
            st[i, 0:HALO, :] = halo_ref[:, cols]
            y = cb_ref[:, cols]
            for j in range(FFN_CONV):
                o = HALO - (FFN_CONV - 1) + j
                y = y + cw_ref[j:j + 1, cols] * st[i, o:o + tm, :]
            halo_ref[:, cols] = st[i, tm:tm + HALO, :]
            ys.append(y.astype(BF16))
        sig = 1.0 / (1.0 + jnp.exp2(ys[0] * (-LOG2E)))
        return ys[0] * sig * ys[1]

    nch = dff // ch
    out_ref[...] = h1
    project(0)
    project(1)
    act = gate(0)
    for c in range(nch):
        if c + 2 < nch:
            project(c + 2)
        act_next = gate(c + 1) if c + 1 < nch else None
        out_ref[...] += jnp.dot(act, wdown_ref[c * ch:(c + 1) * ch, :], preferred_element_type=F32)
        act = act_next
    out_ref[...] = _rms(out_ref[...], g3_ref[...])


def _const_spec(shape):
    return pl.BlockSpec(shape, lambda *_: (0,) * len(shape))


def kernel(x, norm_mix_g, w_in, b_gates, mlstm_conv_w, mlstm_conv_b, att_out_g, mlstm_out_g, w_out, norm_ffn_g, w_up, ffn_conv_w, ffn_conv_b, w_down, norm_final_g):
    B, S, D = x.shape
    H, dh = ATT_HEADS, ATT_HEAD_DIM
    wa = H * dh
    vp = dh + MOBA_VPAD
    wm = MLSTM_HEADS * MLSTM_HEAD_DIM
    L = MOBA_BLOCK
    nb = S // L
    assert S % L == 0 and nb <= 8 and MLSTM_CHUNK == L
    assert w_in.shape[0] == 1 and w_in.shape[2] == 3 * wa + 4 * wm + 2 * MLSTM_HEADS
    dff = w_down.shape[1]
    assert dff % FFN_COL_CHUNK == 0

    wi = w_in[0]
    o_mq, o_mv, o_g = 3 * wa, 3 * wa + 2 * wm, 3 * wa + 4 * wm
    wn = jnp.concatenate([wi[:, wa:2 * wa], wi[:, o_mq:o_mv]], axis=1).astype(BF16)
    wt = jnp.concatenate([wi[:, 0:wa], wi[:, 2 * wa:3 * wa], wi[:, o_mv:o_g]], axis=1).T.astype(BF16)
    nh = MLSTM_HEADS
    wgT = jnp.zeros((16, D), F32).at[0:nh].set(wi[:, o_g:o_g + nh].T).at[8:8 + nh].set(wi[:, o_g + nh:].T).astype(BF16)
    bgT = jnp.zeros((16,), F32).at[0:nh].set(b_gates[0, 0:nh]).at[8:8 + nh].set(b_gates[0, nh:])
    tm = INPROJ_ROW_TILE
    assert S % tm == 0 and tm % L == 0
    bgT = jnp.broadcast_to(bgT.reshape(16, 1), (16, tm))
    g1 = norm_mix_g[0].reshape(1, D)
    cw1 = mlstm_conv_w[0]
    cb1 = mlstm_conv_b[0].reshape(1, 2 * wm)

    nbt = tm // L
    params = pltpu.CompilerParams(dimension_semantics=("arbitrary", "arbitrary"), vmem_limit_bytes=VMEM_LIMIT)
    tile = lambda w: pl.BlockSpec((None, tm, w), lambda b, t: (b, t, 0))
    btile = lambda r, c: pl.BlockSpec((None, nbt, r, c), lambda b, t: (b, t, 0, 0))
    bshape = lambda r, c, dt: jax.ShapeDtypeStruct((B, nb, r, c), dt)
    qT, kb, kmean, vTb, mq, mk, mvT, moT, gT = pl.pallas_call(
        _inproj_kernel,
        grid=(B, S // tm),
        in_specs=[tile(D), _const_spec((1, D)), _const_spec((D, wa + 2 * wm)), _const_spec((2 * wa + 2 * wm, D)),
                  _const_spec((16, D)), _const_spec((16, tm)),
                  _const_spec((MLSTM_CONV, 2 * wm)), _const_spec((1, 2 * wm))],
        out_specs=[btile(wa, L), btile(L, wa), btile(1, wa), btile(H * vp, L),
                   tile(wm), tile(wm), btile(wm, L), btile(wm, L), btile(16, L)],
        out_shape=[bshape(wa, L, BF16), bshape(L, wa, BF16), bshape(1, wa, F32), bshape(H * vp, L, BF16),
                   jax.ShapeDtypeStruct((B, S, wm), BF16),
                   jax.ShapeDtypeStruct((B, S, wm), BF16),
                   bshape(wm, L, BF16), bshape(wm, L, BF16), bshape(16, L, F32)],
        scratch_shapes=[pltpu.VMEM((HALO + tm, 2 * wm), F32)],
        compiler_params=params,
    )(x, g1, wn, wt, wgT, bgT, cw1, cb1)

    km = kmean.reshape(B, nb, H, dh)
    km = jnp.pad(km, ((0, 0), (0, 8 - nb), (0, 0), (0, 0)))
    eye = jnp.eye(H, dtype=F32)
    kmt = jnp.einsum('bjhd,hg->bhjgd', km, eye).reshape(B, H * 8, wa)
    kmh = kmt.astype(BF16)
    kml = (kmt - kmh.astype(F32)).astype(BF16)
    ga = att_out_g[0].reshape(wa, 1)

    att = pl.pallas_call(
        _moba_kernel,
        grid=(B,),
        in_specs=[pl.BlockSpec((None, nb, wa, L), lambda b: (b, 0, 0, 0)),
                  pl.BlockSpec((None, nb, L, wa), lambda b: (b, 0, 0, 0)),
                  pl.BlockSpec((None, nb, H * vp, L), lambda b: (b, 0, 0, 0)),
                  pl.BlockSpec((None, H * 8, wa), lambda b: (b, 0, 0)),
                  pl.BlockSpec((None, H * 8, wa), lambda b: (b, 0, 0)),
                  _const_spec((wa, 1))],
        out_specs=pl.BlockSpec((None, S, wa), lambda b: (b, 0, 0)),
        out_shape=jax.ShapeDtypeStruct((B, S, wa), BF16),
        scratch_shapes=[pltpu.VMEM((H * 2 * dh, L), BF16), pltpu.VMEM((H * 8, L), F32), pltpu.VMEM((wa, L), F32),
                        pltpu.VMEM((H * vp, L), F32), pltpu.VMEM((H, L), F32), pltpu.VMEM((2, H, L), F32),
                        pltpu.VMEM((2, H, L), F32), pltpu.VMEM((2, H, L, L), F32)],
        compiler_params=pltpu.CompilerParams(dimension_semantics=("arbitrary",), vmem_limit_bytes=VMEM_LIMIT),
    )(qT, kb, vTb, kmh, kml, ga)

    seq = lambda w: pl.BlockSpec((None, S, w), lambda b: (b, 0, 0))
    blocks = lambda r: pl.BlockSpec((None, nb, r, L), lambda b: (b, 0, 0, 0))
    gn = jnp.broadcast_to(mlstm_out_g[0].reshape(wm, 1), (wm, L))
    mh = pl.pallas_call(
        _mlstm_kernel,
        grid=(B,),
        in_specs=[seq(wm), seq(wm), blocks(wm), blocks(wm), blocks(16), _const_spec((wm, L))],
        out_specs=seq(wm),
        out_shape=jax.ShapeDtypeStruct((B, S, wm), BF16),
        scratch_shapes=[pltpu.VMEM((MLSTM_HEADS, MLSTM_HEAD_DIM + 8, MLSTM_HEAD_DIM), F32),
                        pltpu.VMEM((MLSTM_HEADS, L, L), F32), pltpu.VMEM((wm, L), F32),
                        pltpu.VMEM((nb, 8, L), F32), pltpu.VMEM((nb, 8, L), F32), pltpu.VMEM((nb, L, 128), F32)],
        compiler_params=pltpu.CompilerParams(dimension_semantics=("arbitrary",), vmem_limit_bytes=VMEM_LIMIT),
    )(mq, mk, mvT, moT, gT, gn)

    tf = FFN_ROW_TILE
    assert S % tf == 0
    ftile = lambda w: pl.BlockSpec((None, tf, w), lambda b, t: (b, t, 0))
    out = pl.pallas_call(
        _ffn_kernel,
        grid=(B, S // tf),
        in_specs=[ftile(D), ftile(wa), ftile(wm), _const_spec((wa + wm, D)), _const_spec((1, D)),
                  _const_spec((D, 2 * dff)), _const_spec((FFN_CONV, 2 * dff)), _const_spec((1, 2 * dff)),
                  _const_spec((dff, D)), _const_spec((1, D))],
        out_specs=ftile(D),
        out_shape=jax.ShapeDtypeStruct((B, S, D), x.dtype),
        scratch_shapes=[pltpu.VMEM((HALO, 2 * dff), F32)]
        + [pltpu.VMEM((2, HALO + tf, FFN_COL_CHUNK), F32) for _ in range(FFN_STAGES)],
        compiler_params=params,
    )(x, att, mh, w_out[0].astype(BF16), norm_ffn_g[0].reshape(1, D), w_up[0].astype(BF16), ffn_conv_w[0],
      ffn_conv_b[0].reshape(1, 2 * dff), w_down[0].astype(BF16), norm_final_g.reshape(1, D))
    return out
```

```python
import jax
import jax.numpy as jnp
from jax import lax
from jax.experimental import pallas as pl
from jax.experimental.pallas import tpu as pltpu

EPS = 1e-6
ATT_HEADS = 8
ATT_HEAD_DIM = 64
MOBA_BLOCK = 256
MOBA_TOPK = 3
MLSTM_HEADS = 4
MLSTM_HEAD_DIM = 128
MLSTM_CONV = 4
FFN_CONV = 3
MLSTM_CHUNK = MOBA_BLOCK
INPROJ_ROW_TILE = 512
INPROJ_CONV_CHUNK = 256
HALO = 8
NEG = -1e30
MOBA_RUN = 8
MOBA_VPAD = 16
LOG2E = 1.4426950408889634
FFN_COL_CHUNK = 256
FFN_ROW_TILE = 512
FFN_STAGES = 3
VMEM_LIMIT = 56 * 1024 * 1024

F32 = jnp.float32
BF16 = jnp.bfloat16

_NT = (((1,), (1,)), ((), ()))


def _sigmoid(y):
    return 1.0 / (1.0 + jnp.exp(-y))


def _rms(x, g):
    return x * lax.rsqrt(jnp.mean(x * x, axis=-1, keepdims=True) + EPS) * g


def _inproj_kernel(x_ref, g_ref, wn_ref, wt_ref, wgT_ref, bgT_ref, cw_ref, cb_ref,
                   qT_ref, k_ref, kmean_ref, vT_ref, mq_ref, mk_ref, mvT_ref, moT_ref, gT_ref,
                   conv_ref):
    t = pl.program_id(1)
    tm = x_ref.shape[0]
    L = MOBA_BLOCK
    wa = k_ref.shape[2]
    wm = mq_ref.shape[1]

    @pl.when(t == 0)
    def _():
        conv_ref[0:HALO, :] = jnp.zeros((HALO, 2 * wm), F32)

    a = _rms(x_ref[...], g_ref[...]).astype(BF16)
    cc = INPROJ_CONV_CHUNK

    def project_conv(j):
        cols = slice(j * cc, (j + 1) * cc)
        conv_ref[HALO:HALO + tm, cols] = jnp.dot(a, wn_ref[:, wa + j * cc:wa + (j + 1) * cc],
                                                 preferred_element_type=F32)

    def finish_conv(j):
        cols = slice(j * cc, (j + 1) * cc)
        y = cb_ref[:, cols]
        for tap in range(MLSTM_CONV):
            off = HALO - (MLSTM_CONV - 1) + tap
            y = y + cw_ref[tap:tap + 1, cols] * conv_ref[off:off + tm, cols]
        conv_ref[0:HALO, cols] = conv_ref[tm:tm + HALO, cols]
        y = y.astype(BF16)
        y = y * (1.0 / (1.0 + jnp.exp2(y * (-LOG2E))))
        if (j + 1) * cc <= wm:
            mq_ref[:, cols] = y * (MLSTM_HEAD_DIM ** -0.5)
        else:
            mk_ref[:, j * cc - wm:(j + 1) * cc - wm] = y

    def project_k():
        k = jnp.dot(a, wn_ref[:, 0:wa], preferred_element_type=F32)
        for i in range(tm // L):
            blk = k[i * L:(i + 1) * L, :]
            k_ref[i] = blk.astype(BF16)
            kmean_ref[i] = jnp.mean(blk, axis=0, keepdims=True)

    def project_transposed():
        vals = lax.dot_general(wt_ref[...], a, _NT, preferred_element_type=F32)
        dh = ATT_HEAD_DIM
        vp = dh + MOBA_VPAD
        ones_row = (lax.broadcasted_iota(jnp.int32, (MOBA_VPAD, L), 0) == 0).astype(BF16)
        for i in range(tm // L):
            blk = slice(i * L, (i + 1) * L)
            qT_ref[i] = (vals[0:wa, blk] * (ATT_HEAD_DIM ** -0.5 * LOG2E)).astype(BF16)
            for h in range(ATT_HEADS):
                vT_ref[i, h * vp:h * vp + dh, :] = vals[wa + h * dh:wa + (h + 1) * dh, blk].astype(BF16)
                vT_ref[i, h * vp + dh:(h + 1) * vp, :] = ones_row
            mvT_ref[i] = vals[2 * wa:2 * wa + wm, blk].astype(BF16)
            moT_ref[i] = vals[2 * wa + wm:2 * wa + 2 * wm, blk].astype(BF16)

    def project_gates():
        gT = lax.dot_general(wgT_ref[...], a, _NT, preferred_element_type=F32) + bgT_ref[...]
        for i in range(tm // L):
            gT_ref[i] = gT[:, i * L:(i + 1) * L]

    others = [project_k, project_transposed, project_gates]
    nconv = 2 * wm // cc
    project_conv(0)
    for j in range(nconv):
        if j + 1 < nconv:
            project_conv(j + 1)
        if others:
            others.pop(0)()
        finish_conv(j)
    for f in others:
        f()


def _moba_block(i, qT_ref, k_ref, vT_ref, kmh_ref, kml_ref, g_ref, o_ref, qz_ref, bias_ref, oT_ref, acc_ref, m_ref,
                alpha_ref, sub_ref, s_ref):
    L = MOBA_BLOCK
    dh = ATT_HEAD_DIM
    nbp = 8
    def prepare():
        qT = qT_ref[i]
        zeros = jnp.zeros((dh, L), BF16)
        for h in range(ATT_HEADS):
            qh = qT[h * dh:(h + 1) * dh, :]
            lo, hi = (qh, zeros) if h % 2 == 0 else (zeros, qh)
            qz_ref[h * 2 * dh:h * 2 * dh + dh, :] = lo
            qz_ref[h * 2 * dh + dh:(h + 1) * 2 * dh, :] = hi
        gate = (jnp.dot(kmh_ref[...], qT, preferred_element_type=F32)
                + jnp.dot(kml_ref[...], qT, preferred_element_type=F32))
        jrow = lax.broadcasted_iota(jnp.int32, (nbp, L), 0)
        past = jrow < i
        for h in range(ATT_HEADS):
            g = gate[h * nbp:(h + 1) * nbp, :]
            gm = jnp.where(past, g, -jnp.inf)
            rank = jnp.zeros((nbp, L), jnp.int32)
            for ii in range(nbp):
                gi = gm[ii:ii + 1, :]
                beats = jnp.where(jrow > ii, jnp.where(gi >= g, 1, 0), jnp.where(gi > g, 1, 0))
                rank = rank + beats
            sel = past & (rank < MOBA_TOPK)
            bias_ref[h * nbp:(h + 1) * nbp, :] = jnp.where(sel, 0.0, NEG)

    krow = lax.broadcasted_iota(jnp.int32, (L, L), 0)
    qcol = lax.broadcasted_iota(jnp.int32, (L, L), 1)
    causal = krow <= qcol

    vp = dh + MOBA_VPAD

    def scores(j, slot, own):
        for p in range(ATT_HEADS // 2):
            k_j = k_ref[j, :, p * 2 * dh:(p + 1) * 2 * dh]
            for h in (2 * p, 2 * p + 1):
                s = jnp.dot(k_j, qz_ref[h * 2 * dh:(h + 1) * 2 * dh, :], preferred_element_type=F32)
                if own:
                    s = jnp.where(causal, s, NEG)
                    m_new = jnp.max(s, axis=0, keepdims=True)
                    alpha = jnp.zeros((1, L), F32)
                    sub = m_new
                else:
                    b = bias_ref[pl.ds(h * nbp + j, 1), :]
                    m = m_ref[h:h + 1, :]
                    m_new = jnp.maximum(m, jnp.max(s, axis=0, keepdims=True) + b)
                    alpha = jnp.exp2(m - m_new)
                    sub = m_new - b
                s_ref[slot, h] = s
                m_ref[h:h + 1, :] = m_new
                alpha_ref[slot, pl.ds(h, 1), :] = alpha
                sub_ref[slot, pl.ds(h, 1), :] = sub

    def values(j, slot):
        for h in range(ATT_HEADS):
            rows = slice(h * vp, (h + 1) * vp)
            pr = jnp.exp2(s_ref[slot, h] - sub_ref[slot, pl.ds(h, 1), :])
            pv = jnp.dot(vT_ref[j, rows, :], pr.astype(BF16), preferred_element_type=F32)
            acc_ref[rows, :] = alpha_ref[slot, pl.ds(h, 1), :] * acc_ref[rows, :] + pv

    def visit_run(blocks):
        scores(blocks[0][0], 0, blocks[0][1])
        for r, (j, _) in enumerate(blocks):
            if r + 1 < len(blocks):
                scores(blocks[r + 1][0], (r + 1) % 2, blocks[r + 1][1])
            values(j, r % 2)

    acc_ref[...] = jnp.zeros(acc_ref.shape, F32)
    tail = lax.rem(i, MOBA_RUN)
    for n in range(MOBA_RUN):
        @pl.when(tail == n)
        def _(n=n):
            prepare()
            visit_run([(i, True)] + [(r, False) for r in range(n)])

    def full_run(t, carry):
        visit_run([(tail + MOBA_RUN * t + r, False) for r in range(MOBA_RUN)])
        return carry

    lax.fori_loop(0, i // MOBA_RUN, full_run, 0)

    for h in range(ATT_HEADS):
        rows = slice(h * dh, (h + 1) * dh)
        acc = acc_ref[h * vp:(h + 1) * vp, :]
        o = acc[0:dh, :] * (1.0 / acc[dh:dh + 1, :])
        oT_ref[rows, :] = o * lax.rsqrt(jnp.mean(o * o, axis=0, keepdims=True) + EPS) * g_ref[rows, :]

    o_ref[pl.ds(pl.multiple_of(i * L, L), L), :] = oT_ref[...].T.astype(BF16)


def _moba_kernel(*refs):
    def qblock(i, carry):
        _moba_block(i, *refs)
        return carry

    lax.fori_loop(0, refs[1].shape[0], qblock, 0)


def _split3(x):
    hi = x.astype(BF16)
    r = x - hi.astype(F32)
    mid = r.astype(BF16)
    lo = (r - mid.astype(F32)).astype(BF16)
    return hi, mid, lo


def _mlstm_kernel(q_ref, k_ref, vT_ref, oT_ref, gT_ref, gn_ref, out_ref, c_ref, s_ref, hT_ref, b_ref, crow_ref,
                  ccol_ref):
    nc, _, L = vT_ref.shape
    D = MLSTM_HEAD_DIM
    H = MLSTM_HEADS
    c_ref[...] = jnp.zeros(c_ref.shape, F32)
    srow = lax.broadcasted_iota(jnp.int32, (L, L), 0)
    tcol = lax.broadcasted_iota(jnp.int32, (L, L), 1)
    causal = srow <= tcol
    upper = causal.astype(BF16)
    first_row = lax.broadcasted_iota(jnp.int32, (8, L), 0) == 0

    for c in range(nc):
        gT = gT_ref[c]
        lfT = (jnp.minimum(gT, 0.0) - jnp.log(1.0 + jnp.exp(-jnp.abs(gT)))) * LOG2E
        bT = sum(jnp.dot(part, upper, preferred_element_type=F32) for part in _split3(lfT))
        c8 = gT[0:8, :] * LOG2E - bT[8:16, :]
        b_ref[c] = bT[8:16, :]
        crow_ref[c] = c8
        ccol_ref[c] = jnp.concatenate([c8, jnp.zeros((128 - 8, L), F32)], axis=0).T

    def chunk(c, ms):
        rows = pl.ds(pl.multiple_of(c * L, L), L)
        b8 = b_ref[c]
        c8 = crow_ref[c]
        c_cols = ccol_ref[c]
        inters = []
        for h in range(H):
            cols = slice(h * D, (h + 1) * D)
            q = q_ref[rows, cols]
            s_ref[h] = lax.dot_general(k_ref[rows, cols], q, _NT, preferred_element_type=F32)
            inters.append(lax.dot_general(c_ref[h].astype(BF16), q, _NT, preferred_element_type=F32))
        new_ms = []
        for h in range(H):
            cols = slice(h * D, (h + 1) * D)
            m_prev = ms[h]
            b_row = b8[h:h + 1, :]
            c_row = c8[h:h + 1, :]
            cm = jnp.where(causal, c_cols[:, h:h + 1], NEG)
            mx = jnp.maximum(m_prev, jnp.max(cm, axis=0, keepdims=True))
            w_inter = jnp.exp2(m_prev - mx)
            sT = s_ref[h] * jnp.exp2(cm - mx)
            vT = vT_ref[c, cols, :]
            inter = inters[h]
            num = w_inter * inter[0:D, :] + jnp.dot(vT, sT.astype(BF16), preferred_element_type=F32)
            den = w_inter * inter[D:D + 1, :] + jnp.sum(sT, axis=0, keepdims=True)
            m_t = b_row + mx
            hid = num * (1.0 / jnp.maximum(jnp.abs(den), jnp.exp2(-m_t)))
            hid = hid * lax.rsqrt(jnp.mean(hid * hid, axis=0, keepdims=True) + EPS) * gn_ref[cols, :]
            hT_ref[cols, :] = hid * _sigmoid(oT_ref[c, cols, :].astype(F32))

            m_last = m_t[:, L - 1:L]
            decay = w_inter[:, L - 1:L]
            w_last = jnp.exp2(c_row + (b_row[:, L - 1:L] - m_last))
            lhs = jnp.concatenate([vT.astype(F32) * w_last,
                                   jnp.where(first_row, w_last, 0.0)], axis=0).astype(BF16)
            c_ref[h] = decay * c_ref[h] + jnp.dot(lhs, k_ref[rows, cols], preferred_element_type=F32)
            new_ms.append(m_last)
        out_ref[rows, :] = hT_ref[...].T.astype(out_ref.dtype)
        return tuple(new_ms)

    lax.fori_loop(0, nc, chunk, tuple(jnp.zeros((1, 1), F32) for _ in range(H)))


def _ffn_kernel(x_ref, att_ref, mh_ref, wout_ref, g2_ref, wup_ref, cw_ref, cb_ref, wdown_ref, g3_ref,
                out_ref, halo_ref, *stage_refs):
    t = pl.program_id(1)
    tm = x_ref.shape[0]
    dff = wdown_ref.shape[0]
    ch = FFN_COL_CHUNK
    nst = len(stage_refs)

    heads = jnp.concatenate([att_ref[...], mh_ref[...]], axis=1)
    h1 = x_ref[...] + jnp.dot(heads, wout_ref[...], preferred_element_type=F32)
    a2 = _rms(h1, g2_ref[...]).astype(BF16)

    @pl.when(t == 0)
    def _():
        halo_ref[...] = jnp.zeros(halo_ref.shape, F32)

    def project(c):
        st = stage_refs[c % nst]
        for i, off in enumerate((c * ch, dff + c * ch)):
            st[i, HALO:HALO + tm, :] = jnp.dot(a2, wup_ref[:, off:off + ch], preferred_element_type=F32)

    def gate(c):
        st = stage_refs[c % nst]
        ys = []
        for i, off in enumerate((c * ch, dff + c * ch)):
            cols = slice(off, off + ch)
            st[i, 0:HALO, :] = halo_ref[:, cols]
            y = cb_ref[:, cols]
            for j in range(FFN_CONV):
                o = HALO - (FFN_CONV - 1) + j
                y = y + cw_ref[j:j + 1, cols] * st[i, o:o + tm, :]
            halo_ref[:, cols] = st[i, tm:tm + HALO, :]
            ys.append(y.astype(BF16))
        sig = 1.0 / (1.0 + jnp.exp2(ys[0] * (-LOG2E)))
        return ys[0] * sig * ys[1]

    nch = dff // ch
    out_ref[...] = h1
    project(0)
    project(1)
    act = gate(0)
    for c in range(nch):
        if c + 2 < nch:
            project(c + 2)
        act_next = gate(c + 1) if c + 1 < nch else None
        out_ref[...] += jnp.dot(act, wdown_ref[c * ch:(c + 1) * ch, :], preferred_element_type=F32)
        act = act_next
    out_ref[...] = _rms(out_ref[...], g3_ref[...])


def _const_spec(shape):
    return pl.BlockSpec(shape, lambda *_: (0,) * len(shape))


def kernel(x, norm_mix_g, w_in, b_gates, mlstm_conv_w, mlstm_conv_b, att_out_g, mlstm_out_g, w_out, norm_ffn_g, w_up, ffn_conv_w, ffn_conv_b, w_down, norm_final_g):
    B, S, D = x.shape
    H, dh = ATT_HEADS, ATT_HEAD_DIM
    wa = H * dh
    vp = dh + MOBA_VPAD
    wm = MLSTM_HEADS * MLSTM_HEAD_DIM
    L = MOBA_BLOCK
    nb = S // L
    assert S % L == 0 and nb <= 8 and MLSTM_CHUNK == L
    assert w_in.shape[0] == 1 and w_in.shape[2] == 3 * wa + 4 * wm + 2 * MLSTM_HEADS
    dff = w_down.shape[1]
    assert dff % FFN_COL_CHUNK == 0

    wi = w_in[0]
    o_mq, o_mv, o_g = 3 * wa, 3 * wa + 2 * wm, 3 * wa + 4 * wm
    wn = jnp.concatenate([wi[:, wa:2 * wa], wi[:, o_mq:o_mv]], axis=1).astype(BF16)
    wt = jnp.concatenate([wi[:, 0:wa], wi[:, 2 * wa:3 * wa], wi[:, o_mv:o_g]], axis=1).T.astype(BF16)
    nh = MLSTM_HEADS
    wgT = jnp.zeros((16, D), F32).at[0:nh].set(wi[:, o_g:o_g + nh].T).at[8:8 + nh].set(wi[:, o_g + nh:].T).astype(BF16)
    bgT = jnp.zeros((16,), F32).at[0:nh].set(b_gates[0, 0:nh]).at[8:8 + nh].set(b_gates[0, nh:])
    tm = INPROJ_ROW_TILE
    assert S % tm == 0 and tm % L == 0
    bgT = jnp.broadcast_to(bgT.reshape(16, 1), (16, tm))
    g1 = norm_mix_g[0].reshape(1, D)
    cw1 = mlstm_conv_w[0]
    cb1 = mlstm_conv_b[0].reshape(1, 2 * wm)

    nbt = tm // L
    params = pltpu.CompilerParams(dimension_semantics=("arbitrary", "arbitrary"), vmem_limit_bytes=VMEM_LIMIT)
    tile = lambda w: pl.BlockSpec((None, tm, w), lambda b, t: (b, t, 0))
    btile = lambda r, c: pl.BlockSpec((None, nbt, r, c), lambda b, t: (b, t, 0, 0))
    bshape = lambda r, c, dt: jax.ShapeDtypeStruct((B, nb, r, c), dt)
    qT, kb, kmean, vTb, mq, mk, mvT, moT, gT = pl.pallas_call(
        _inproj_kernel,
        grid=(B, S // tm),
        in_specs=[tile(D), _const_spec((1, D)), _const_spec((D, wa + 2 * wm)), _const_spec((2 * wa + 2 * wm, D)),
                  _const_spec((16, D)), _const_spec((16, tm)),
                  _const_spec((MLSTM_CONV, 2 * wm)), _const_spec((1, 2 * wm))],
        out_specs=[btile(wa, L), btile(L, wa), btile(1, wa), btile(H * vp, L),
                   tile(wm), tile(wm), btile(wm, L), btile(wm, L), btile(16, L)],
        out_shape=[bshape(wa, L, BF16), bshape(L, wa, BF16), bshape(1, wa, F32), bshape(H * vp, L, BF16),
                   jax.ShapeDtypeStruct((B, S, wm), BF16),
                   jax.ShapeDtypeStruct((B, S, wm), BF16),
                   bshape(wm, L, BF16), bshape(wm, L, BF16), bshape(16, L, F32)],
        scratch_shapes=[pltpu.VMEM((HALO + tm, 2 * wm), F32)],
        compiler_params=params,
    )(x, g1, wn, wt, wgT, bgT, cw1, cb1)

    km = kmean.reshape(B, nb, H, dh)
    km = jnp.pad(km, ((0, 0), (0, 8 - nb), (0, 0), (0, 0)))
    eye = jnp.eye(H, dtype=F32)
    kmt = jnp.einsum('bjhd,hg->bhjgd', km, eye).reshape(B, H * 8, wa)
    kmh = kmt.astype(BF16)
    kml = (kmt - kmh.astype(F32)).astype(BF16)
    ga = att_out_g[0].reshape(wa, 1)

    att = pl.pallas_call(
        _moba_kernel,
        grid=(B,),
        in_specs=[pl.BlockSpec((None, nb, wa, L), lambda b: (b, 0, 0, 0)),
                  pl.BlockSpec((None, nb, L, wa), lambda b: (b, 0, 0, 0)),
                  pl.BlockSpec((None, nb, H * vp, L), lambda b: (b, 0, 0, 0)),
                  pl.BlockSpec((None, H * 8, wa), lambda b: (b, 0, 0)),
                  pl.BlockSpec((None, H * 8, wa), lambda b: (b, 0, 0)),
                  _const_spec((wa, 1))],
        out_specs=pl.BlockSpec((None, S, wa), lambda b: (b, 0, 0)),
        out_shape=jax.ShapeDtypeStruct((B, S, wa), BF16),
        scratch_shapes=[pltpu.VMEM((H * 2 * dh, L), BF16), pltpu.VMEM((H * 8, L), F32), pltpu.VMEM((wa, L), F32),
                        pltpu.VMEM((H * vp, L), F32), pltpu.VMEM((H, L), F32), pltpu.VMEM((2, H, L), F32),
                        pltpu.VMEM((2, H, L), F32), pltpu.VMEM((2, H, L, L), F32)],
        compiler_params=pltpu.CompilerParams(dimension_semantics=("arbitrary",), vmem_limit_bytes=VMEM_LIMIT),
    )(qT, kb, vTb, kmh, kml, ga)

    seq = lambda w: pl.BlockSpec((None, S, w), lambda b: (b, 0, 0))
    blocks = lambda r: pl.BlockSpec((None, nb, r, L), lambda b: (b, 0, 0, 0))
    gn = jnp.broadcast_to(mlstm_out_g[0].reshape(wm, 1), (wm, L))
    mh = pl.pallas_call(
        _mlstm_kernel,
        grid=(B,),
        in_specs=[seq(wm), seq(wm), blocks(wm), blocks(wm), blocks(16), _const_spec((wm, L))],
        out_specs=seq(wm),
        out_shape=jax.ShapeDtypeStruct((B, S, wm), BF16),
        scratch_shapes=[pltpu.VMEM((MLSTM_HEADS, MLSTM_HEAD_DIM + 8, MLSTM_HEAD_DIM), F32),
                        pltpu.VMEM((MLSTM_HEADS, L, L), F32), pltpu.VMEM((wm, L), F32),
                        pltpu.VMEM((nb, 8, L), F32), pltpu.VMEM((nb, 8, L), F32), pltpu.VMEM((nb, L, 128), F32)],
        compiler_params=pltpu.CompilerParams(dimension_semantics=("arbitrary",), vmem_limit_bytes=VMEM_LIMIT),
    )(mq, mk, mvT, moT, gT, gn)

    tf = FFN_ROW_TILE
    assert S % tf == 0
    ftile = lambda w: pl.BlockSpec((None, tf, w), lambda b, t: (b, t, 0))
    out = pl.pallas_call(
        _ffn_kernel,
        grid=(B, S // tf),
        in_specs=[ftile(D), ftile(wa), ftile(wm), _const_spec((wa + wm, D)), _const_spec((1, D)),
                  _const_spec((D, 2 * dff)), _const_spec((FFN_CONV, 2 * dff)), _const_spec((1, 2 * dff)),
                  _const_spec((dff, D)), _const_spec((1, D))],
        out_specs=ftile(D),
        out_shape=jax.ShapeDtypeStruct((B, S, D), x.dtype),
        scratch_shapes=[pltpu.VMEM((HALO, 2 * dff), F32)]
        + [pltpu.VMEM((2, HALO + tf, FFN_COL_CHUNK), F32) for _ in range(FFN_STAGES)],
        compiler_params=params,
    )(x, att, mh, w_out[0].astype(BF16), norm_ffn_g[0].reshape(1, D), w_up[0].astype(BF16), ffn_conv_w[0],
      ffn_conv_b[0].reshape(1, 2 * dff), w_down[0].astype(BF16), norm_final_g.reshape(1, D))
    return out
```

```python
import jax
import jax.numpy as jnp
from jax import lax
from jax.experimental import pallas as pl
from jax.experimental.pallas import tpu as pltpu

EPS = 1e-6
ATT_HEADS = 8
ATT_HEAD_DIM = 64
MOBA_BLOCK = 256
MOBA_TOPK = 3
MLSTM_HEADS = 4
MLSTM_HEAD_DIM = 128
MLSTM_CONV = 4
FFN_CONV = 3
MLSTM_CHUNK = MOBA_BLOCK
INPROJ_ROW_TILE = 1024
INPROJ_CONV_CHUNK = 256
HALO = 8
NEG = -1e30
MOBA_RUN = 8
MOBA_VPAD = 16
LOG2E = 1.4426950408889634
FFN_COL_CHUNK = 256
FFN_ROW_TILE = 512
FFN_STAGES = 3
VMEM_LIMIT = 56 * 1024 * 1024

F32 = jnp.float32
BF16 = jnp.bfloat16

_NT = (((1,), (1,)), ((), ()))


def _sigmoid(y):
    return 1.0 / (1.0 + jnp.exp(-y))


def _rms(x, g):
    return x * lax.rsqrt(jnp.mean(x * x, axis=-1, keepdims=True) + EPS) * g


def _inproj_kernel(x_ref, g_ref, wn_ref, wt_ref, wgT_ref, bgT_ref, cw_ref, cb_ref,
                   qT_ref, k_ref, kmean_ref, vT_ref, mq_ref, mk_ref, mvT_ref, moT_ref, gT_ref,
                   conv_ref):
    t = pl.program_id(1)
    tm = x_ref.shape[0]
    L = MOBA_BLOCK
    wa = k_ref.shape[2]
    wm = mq_ref.shape[1]

    @pl.when(t == 0)
    def _():
        conv_ref[0:HALO, :] = jnp.zeros((HALO, 2 * wm), F32)

    a = _rms(x_ref[...], g_ref[...]).astype(BF16)
    cc = INPROJ_CONV_CHUNK

    def project_conv(j):
        cols = slice(j * cc, (j + 1) * cc)
        conv_ref[HALO:HALO + tm, cols] = jnp.dot(a, wn_ref[:, wa + j * cc:wa + (j + 1) * cc],
                                                 preferred_element_type=F32)

    def finish_conv(j):
        cols = slice(j * cc, (j + 1) * cc)
        y = cb_ref[:, cols]
        for tap in range(MLSTM_CONV):
            off = HALO - (MLSTM_CONV - 1) + tap
            y = y + cw_ref[tap:tap + 1, cols] * conv_ref[off:off + tm, cols]
        conv_ref[0:HALO, cols] = conv_ref[tm:tm + HALO, cols]
        y = y.astype(BF16)
        y = y * (1.0 / (1.0 + jnp.exp2(y * (-LOG2E))))
        if (j + 1) * cc <= wm:
            mq_ref[:, cols] = y * (MLSTM_HEAD_DIM ** -0.5)
        else:
            mk_ref[:, j * cc - wm:(j + 1) * cc - wm] = y

    def project_k():
        k = jnp.dot(a, wn_ref[:, 0:wa], preferred_element_type=F32)
        for i in range(tm // L):
            blk = k[i * L:(i + 1) * L, :]
            k_ref[i] = blk.astype(BF16)
            kmean_ref[i] = jnp.mean(blk, axis=0, keepdims=True)

    def transposed(lo, hi):
        return lax.dot_general(wt_ref[lo:hi, :], a, _NT, preferred_element_type=F32)

    def project_qT():
        vals = transposed(0, wa) * (ATT_HEAD_DIM ** -0.5 * LOG2E)
        for i in range(tm // L):
            qT_ref[i] = vals[:, i * L:(i + 1) * L].astype(BF16)

    def project_blocks(ref, lo, hi):
        vals = transposed(lo, hi)
        for i in range(tm // L):
            ref[i] = vals[:, i * L:(i + 1) * L].astype(BF16)

    def project_vT():
        vals = transposed(wa, 2 * wa)
        dh = ATT_HEAD_DIM
        vp = dh + MOBA_VPAD
        ones_row = (lax.broadcasted_iota(jnp.int32, (MOBA_VPAD, L), 0) == 0).astype(BF16)
        for i in range(tm // L):
            for h in range(ATT_HEADS):
                vT_ref[i, h * vp:h * vp + dh, :] = vals[h * dh:(h + 1) * dh, i * L:(i + 1) * L].astype(BF16)
                vT_ref[i, h * vp + dh:(h + 1) * vp, :] = ones_row

    def project_gates():
        gT = lax.dot_general(wgT_ref[...], a, _NT, preferred_element_type=F32) + bgT_ref[...]
        for i in range(tm // L):
            gT_ref[i] = gT[:, i * L:(i + 1) * L]

    others = [project_k, project_qT, project_vT,
              lambda: project_blocks(mvT_ref, 2 * wa, 2 * wa + wm),
              lambda: project_blocks(moT_ref, 2 * wa + wm, 2 * wa + 2 * wm), project_gates]
    nconv = 2 * wm // cc
    project_conv(0)
    for j in range(nconv):
        if j + 1 < nconv:
            project_conv(j + 1)
        if others:
            others.pop(0)()
        finish_conv(j)
    for f in others:
        f()


def _moba_block(i, qT_ref, k_ref, vT_ref, kmh_ref, kml_ref, g_ref, o_ref, qz_ref, bias_ref, oT_ref, acc_ref, m_ref,
                alpha_ref, sub_ref, s_ref):
    L = MOBA_BLOCK
    dh = ATT_HEAD_DIM
    nbp = 8
    def prepare():
        qT = qT_ref[i]
        zeros = jnp.zeros((dh, L), BF16)
        for h in range(ATT_HEADS):
            qh = qT[h * dh:(h + 1) * dh, :]
            lo, hi = (qh, zeros) if h % 2 == 0 else (zeros, qh)
            qz_ref[h * 2 * dh:h * 2 * dh + dh, :] = lo
            qz_ref[h * 2 * dh + dh:(h + 1) * 2 * dh, :] = hi
        gate = (jnp.dot(kmh_ref[...], qT, preferred_element_type=F32)
                + jnp.dot(kml_ref[...], qT, preferred_element_type=F32))
        jrow = lax.broadcasted_iota(jnp.int32, (nbp, L), 0)
        past = jrow < i
        for h in range(ATT_HEADS):
            g = gate[h * nbp:(h + 1) * nbp, :]
            gm = jnp.where(past, g, -jnp.inf)
            rank = jnp.zeros((nbp, L), jnp.int32)
            for ii in range(nbp):
                gi = gm[ii:ii + 1, :]
                beats = jnp.where(jrow > ii, jnp.where(gi >= g, 1, 0), jnp.where(gi > g, 1, 0))
                rank = rank + beats
            sel = past & (rank < MOBA_TOPK)
            bias_ref[h * nbp:(h + 1) * nbp, :] = jnp.where(sel, 0.0, NEG)

    krow = lax.broadcasted_iota(jnp.int32, (L, L), 0)
    qcol = lax.broadcasted_iota(jnp.int32, (L, L), 1)
    causal = krow <= qcol

    vp = dh + MOBA_VPAD

    def scores(j, slot, own):
        for p in range(ATT_HEADS // 2):
            k_j = k_ref[j, :, p * 2 * dh:(p + 1) * 2 * dh]
            for h in (2 * p, 2 * p + 1):
                s = jnp.dot(k_j, qz_ref[h * 2 * dh:(h + 1) * 2 * dh, :], preferred_element_type=F32)
                if own:
                    s = jnp.where(causal, s, NEG)
                    m_new = jnp.max(s, axis=0, keepdims=True)
                    alpha = jnp.zeros((1, L), F32)
                    sub = m_new
                else:
                    b = bias_ref[pl.ds(h * nbp + j, 1), :]
                    m = m_ref[h:h + 1, :]
                    m_new = jnp.maximum(m, jnp.max(s, axis=0, keepdims=True) + b)
                    alpha = jnp.exp2(m - m_new)
                    sub = m_new - b
                s_ref[slot, h] = s
                m_ref[h:h + 1, :] = m_new
                alpha_ref[slot, pl.ds(h, 1), :] = alpha
                sub_ref[slot, pl.ds(h, 1), :] = sub

    def values(j, slot):
        for h in range(ATT_HEADS):
            rows = slice(h * vp, (h + 1) * vp)
            pr = jnp.exp2(s_ref[slot, h] - sub_ref[slot, pl.ds(h, 1), :])
            pv = jnp.dot(vT_ref[j, rows, :], pr.astype(BF16), preferred_element_type=F32)
            acc_ref[rows, :] = alpha_ref[slot, pl.ds(h, 1), :] * acc_ref[rows, :] + pv

    def visit_run(blocks):
        scores(blocks[0][0], 0, blocks[0][1])
        for r, (j, _) in enumerate(blocks):
            if r + 1 < len(blocks):
                scores(blocks[r + 1][0], (r + 1) % 2, blocks[r + 1][1])
            values(j, r % 2)

    acc_ref[...] = jnp.zeros(acc_ref.shape, F32)
    tail = lax.rem(i, MOBA_RUN)
    for n in range(MOBA_RUN):
        @pl.when(tail == n)
        def _(n=n):
            prepare()
            visit_run([(i, True)] + [(r, False) for r in range(n)])

    def full_run(t, carry):
        visit_run([(tail + MOBA_RUN * t + r, False) for r in range(MOBA_RUN)])
        return carry

    lax.fori_loop(0, i // MOBA_RUN, full_run, 0)

    for h in range(ATT_HEADS):
        rows = slice(h * dh, (h + 1) * dh)
        acc = acc_ref[h * vp:(h + 1) * vp, :]
        o = acc[0:dh, :] * (1.0 / acc[dh:dh + 1, :])
        oT_ref[rows, :] = o * lax.rsqrt(jnp.mean(o * o, axis=0, keepdims=True) + EPS) * g_ref[rows, :]

    o_ref[pl.ds(pl.multiple_of(i * L, L), L), :] = oT_ref[...].T.astype(BF16)


def _moba_kernel(*refs):
    def qblock(i, carry):
        _moba_block(i, *refs)
        return carry

    lax.fori_loop(0, refs[1].shape[0], qblock, 0)


def _split3(x):
    hi = x.astype(BF16)
    r = x - hi.astype(F32)
    mid = r.astype(BF16)
    lo = (r - mid.astype(F32)).astype(BF16)
    return hi, mid, lo


def _mlstm_kernel(q_ref, k_ref, vT_ref, oT_ref, gT_ref, gn_ref, out_ref, c_ref, s_ref, hT_ref, b_ref, crow_ref,
                  ccol_ref):
    nc, _, L = vT_ref.shape
    D = MLSTM_HEAD_DIM
    H = MLSTM_HEADS
    c_ref[...] = jnp.zeros(c_ref.shape, F32)
    srow = lax.broadcasted_iota(jnp.int32, (L, L), 0)
    tcol = lax.broadcasted_iota(jnp.int32, (L, L), 1)
    causal = srow <= tcol
    upper = causal.astype(BF16)
    first_row = lax.broadcasted_iota(jnp.int32, (8, L), 0) == 0

    for c in range(nc):
        gT = gT_ref[c]
        lfT = (jnp.minimum(gT, 0.0) - jnp.log(1.0 + jnp.exp(-jnp.abs(gT)))) * LOG2E
        bT = sum(jnp.dot(part, upper, preferred_element_type=F32) for part in _split3(lfT))
        c8 = gT[0:8, :] * LOG2E - bT[8:16, :]
        b_ref[c] = bT[8:16, :]
        crow_ref[c] = c8
        ccol_ref[c] = jnp.concatenate([c8, jnp.zeros((128 - 8, L), F32)], axis=0).T

    def chunk(c, ms):
        rows = pl.ds(pl.multiple_of(c * L, L), L)
        b8 = b_ref[c]
        c8 = crow_ref[c]
        c_cols = ccol_ref[c]
        inters = []
        for h in range(H):
            cols = slice(h * D, (h + 1) * D)
            q = q_ref[rows, cols]
            s_ref[h] = lax.dot_general(k_ref[rows, cols], q, _NT, preferred_element_type=F32)
            inters.append(lax.dot_general(c_ref[h].astype(BF16), q, _NT, preferred_element_type=F32))
        new_ms = []
        for h in range(H):
            cols = slice(h * D, (h + 1) * D)
            m_prev = ms[h]
            b_row = b8[h:h + 1, :]
            c_row = c8[h:h + 1, :]
            cm = jnp.where(causal, c_cols[:, h:h + 1], NEG)
            mx = jnp.maximum(m_prev, jnp.max(cm, axis=0, keepdims=True))
            w_inter = jnp.exp2(m_prev - mx)
            sT = s_ref[h] * jnp.exp2(cm - mx)
            vT = vT_ref[c, cols, :]
            inter = inters[h]
            num = w_inter * inter[0:D, :] + jnp.dot(vT, sT.astype(BF16), preferred_element_type=F32)
            den = w_inter * inter[D:D + 1, :] + jnp.sum(sT, axis=0, keepdims=True)
            m_t = b_row + mx
            hid = num * (1.0 / jnp.maximum(jnp.abs(den), jnp.exp2(-m_t)))
            hid = hid * lax.rsqrt(jnp.mean(hid * hid, axis=0, keepdims=True) + EPS) * gn_ref[cols, :]
            hT_ref[cols, :] = hid * _sigmoid(oT_ref[c, cols, :].astype(F32))

            m_last = m_t[:, L - 1:L]
            decay = w_inter[:, L - 1:L]
            w_last = jnp.exp2(c_row + (b_row[:, L - 1:L] - m_last))
            lhs = jnp.concatenate([vT.astype(F32) * w_last,
                                   jnp.where(first_row, w_last, 0.0)], axis=0).astype(BF16)
            c_ref[h] = decay * c_ref[h] + jnp.dot(lhs, k_ref[rows, cols], preferred_element_type=F32)
            new_ms.append(m_last)
        out_ref[rows, :] = hT_ref[...].T.astype(out_ref.dtype)
        return tuple(new_ms)

    lax.fori_loop(0, nc, chunk, tuple(jnp.zeros((1, 1), F32) for _ in range(H)))


def _ffn_kernel(x_ref, att_ref, mh_ref, wout_ref, g2_ref, wup_ref, cw_ref, cb_ref, wdown_ref, g3_ref,
                out_ref, halo_ref, *stage_refs):
    t = pl.program_id(1)
    tm = x_ref.shape[0]
    dff = wdown_ref.shape[0]
    ch = FFN_COL_CHUNK
    nst = len(stage_refs)

    heads = jnp.concatenate([att_ref[...], mh_ref[...]], axis=1)
    h1 = x_ref[...] + jnp.dot(heads, wout_ref[...], preferred_element_type=F32)
    a2 = _rms(h1, g2_ref[...]).astype(BF16)

    @pl.when(t == 0)
    def _():
        halo_ref[...] = jnp.zeros(halo_ref.shape, F32)

    def project(c):
        st = stage_refs[c % nst]
        for i, off in enumerate((c * ch, dff + c * ch)):
            st[i, HALO:HALO + tm, :] = jnp.dot(a2, wup_ref[:, off:off + ch], preferred_element_type=F32)

    def gate(c):
        st = stage_refs[c % nst]
        ys = []
        for i, off in enumerate((c * ch, dff + c * ch)):
            cols = slice(off, off + ch)
            st[i, 0:HALO, :] = halo_ref[:, cols]
            y = cb_ref[:, cols]
            for j in range(FFN_CONV):
                o = HALO - (FFN_CONV - 1) + j
                y = y + cw_ref[j:j + 1, cols] * st[i, o:o + tm, :]
            halo_ref[:, cols] = st[i, tm:tm + HALO, :]
            ys.append(y.astype(BF16))
        sig = 1.0 / (1.0 + jnp.exp2(ys[0] * (-LOG2E)))
        return ys[0] * sig * ys[1]

    nch = dff // ch
    out_ref[...] = h1
    project(0)
    project(1)
    act = gate(0)
    for c in range(nch):
        if c + 2 < nch:
            project(c + 2)
        act_next = gate(c + 1) if c + 1 < nch else None
        out_ref[...] += jnp.dot(act, wdown_ref[c * ch:(c + 1) * ch, :], preferred_element_type=F32)
        act = act_next
    out_ref[...] = _rms(out_ref[...], g3_ref[...])


def _const_spec(shape):
    return pl.BlockSpec(shape, lambda *_: (0,) * len(shape))


def kernel(x, norm_mix_g, w_in, b_gates, mlstm_conv_w, mlstm_conv_b, att_out_g, mlstm_out_g, w_out, norm_ffn_g, w_up, ffn_conv_w, ffn_conv_b, w_down, norm_final_g):
    B, S, D = x.shape
    H, dh = ATT_HEADS, ATT_HEAD_DIM
    wa = H * dh
    vp = dh + MOBA_VPAD
    wm = MLSTM_HEADS * MLSTM_HEAD_DIM
    L = MOBA_BLOCK
    nb = S // L
    assert S % L == 0 and nb <= 8 and MLSTM_CHUNK == L
    assert w_in.shape[0] == 1 and w_in.shape[2] == 3 * wa + 4 * wm + 2 * MLSTM_HEADS
    dff = w_down.shape[1]
    assert dff % FFN_COL_CHUNK == 0

    wi = w_in[0]
    o_mq, o_mv, o_g = 3 * wa, 3 * wa + 2 * wm, 3 * wa + 4 * wm
    wn = jnp.concatenate([wi[:, wa:2 * wa], wi[:, o_mq:o_mv]], axis=1).astype(BF16)
    wt = jnp.concatenate([wi[:, 0:wa], wi[:, 2 * wa:3 * wa], wi[:, o_mv:o_g]], axis=1).T.astype(BF16)
    nh = MLSTM_HEADS
    wgT = jnp.zeros((16, D), F32).at[0:nh].set(wi[:, o_g:o_g + nh].T).at[8:8 + nh].set(wi[:, o_g + nh:].T).astype(BF16)
    bgT = jnp.zeros((16,), F32).at[0:nh].set(b_gates[0, 0:nh]).at[8:8 + nh].set(b_gates[0, nh:])
    tm = INPROJ_ROW_TILE
    assert S % tm == 0 and tm % L == 0
    bgT = jnp.broadcast_to(bgT.reshape(16, 1), (16, tm))
    g1 = norm_mix_g[0].reshape(1, D)
    cw1 = mlstm_conv_w[0]
    cb1 = mlstm_conv_b[0].reshape(1, 2 * wm)

    nbt = tm // L
    params = pltpu.CompilerParams(dimension_semantics=("arbitrary", "arbitrary"), vmem_limit_bytes=VMEM_LIMIT)
    tile = lambda w: pl.BlockSpec((None, tm, w), lambda b, t: (b, t, 0))
    btile = lambda r, c: pl.BlockSpec((None, nbt, r, c), lambda b, t: (b, t, 0, 0))
    bshape = lambda r, c, dt: jax.ShapeDtypeStruct((B, nb, r, c), dt)
    qT, kb, kmean, vTb, mq, mk, mvT, moT, gT = pl.pallas_call(
        _inproj_kernel,
        grid=(B, S // tm),
        in_specs=[tile(D), _const_spec((1, D)), _const_spec((D, wa + 2 * wm)), _const_spec((2 * wa + 2 * wm, D)),
                  _const_spec((16, D)), _const_spec((16, tm)),
                  _const_spec((MLSTM_CONV, 2 * wm)), _const_spec((1, 2 * wm))],
        out_specs=[btile(wa, L), btile(L, wa), btile(1, wa), btile(H * vp, L),
                   tile(wm), tile(wm), btile(wm, L), btile(wm, L), btile(16, L)],
        out_shape=[bshape(wa, L, BF16), bshape(L, wa, BF16), bshape(1, wa, F32), bshape(H * vp, L, BF16),
                   jax.ShapeDtypeStruct((B, S, wm), BF16),
                   jax.ShapeDtypeStruct((B, S, wm), BF16),
                   bshape(wm, L, BF16), bshape(wm, L, BF16), bshape(16, L, F32)],
        scratch_shapes=[pltpu.VMEM((HALO + tm, 2 * wm), F32)],
        compiler_params=params,
    )(x, g1, wn, wt, wgT, bgT, cw1, cb1)

    km = kmean.reshape(B, nb, H, dh)
    km = jnp.pad(km, ((0, 0), (0, 8 - nb), (0, 0), (0, 0)))
    eye = jnp.eye(H, dtype=F32)
    kmt = jnp.einsum('bjhd,hg->bhjgd', km, eye).reshape(B, H * 8, wa)
    kmh = kmt.astype(BF16)
    kml = (kmt - kmh.astype(F32)).astype(BF16)
    ga = att_out_g[0].reshape(wa, 1)

    att = pl.pallas_call(
        _moba_kernel,
        grid=(B,),
        in_specs=[pl.BlockSpec((None, nb, wa, L), lambda b: (b, 0, 0, 0)),
                  pl.BlockSpec((None, nb, L, wa), lambda b: (b, 0, 0, 0)),
                  pl.BlockSpec((None, nb, H * vp, L), lambda b: (b, 0, 0, 0)),
                  pl.BlockSpec((None, H * 8, wa), lambda b: (b, 0, 0)),
                  pl.BlockSpec((None, H * 8, wa), lambda b: (b, 0, 0)),
                  _const_spec((wa, 1))],
        out_specs=pl.BlockSpec((None, S, wa), lambda b: (b, 0, 0)),
        out_shape=jax.ShapeDtypeStruct((B, S, wa), BF16),
        scratch_shapes=[pltpu.VMEM((H * 2 * dh, L), BF16), pltpu.VMEM((H * 8, L), F32), pltpu.VMEM((wa, L), F32),
                        pltpu.VMEM((H * vp, L), F32), pltpu.VMEM((H, L), F32), pltpu.VMEM((2, H, L), F32),
                        pltpu.VMEM((2, H, L), F32), pltpu.VMEM((2, H, L, L), F32)],
        compiler_params=pltpu.CompilerParams(dimension_semantics=("arbitrary",), vmem_limit_bytes=VMEM_LIMIT),
    )(qT, kb, vTb, kmh, kml, ga)

    seq = lambda w: pl.BlockSpec((None, S, w), lambda b: (b, 0, 0))
    blocks = lambda r: pl.BlockSpec((None, nb, r, L), lambda b: (b, 0, 0, 0))
    gn = jnp.broadcast_to(mlstm_out_g[0].reshape(wm, 1), (wm, L))
    mh = pl.pallas_call(
        _mlstm_kernel,
        grid=(B,),
        in_specs=[seq(wm), seq(wm), blocks(wm), blocks(wm), blocks(16), _const_spec((wm, L))],
        out_specs=seq(wm),
        out_shape=jax.ShapeDtypeStruct((B, S, wm), BF16),
        scratch_shapes=[pltpu.VMEM((MLSTM_HEADS, MLSTM_HEAD_DIM + 8, MLSTM_HEAD_DIM), F32),
                        pltpu.VMEM((MLSTM_HEADS, L, L), F32), pltpu.VMEM((wm, L), F32),
                        pltpu.VMEM((nb, 8, L), F32), pltpu.VMEM((nb, 8, L), F32), pltpu.VMEM((nb, L, 128), F32)],
        compiler_params=pltpu.CompilerParams(dimension_semantics=("arbitrary",), vmem_limit_bytes=VMEM_LIMIT),
    )(mq, mk, mvT, moT, gT, gn)

    tf = FFN_ROW_TILE
    assert S % tf == 0
    ftile = lambda w: pl.BlockSpec((None, tf, w), lambda b, t: (b, t, 0))
    out = pl.pallas_call(
        _ffn_kernel,
        grid=(B, S // tf),
        in_specs=[ftile(D), ftile(wa), ftile(wm), _const_spec((wa + wm, D)), _const_spec((1, D)),
                  _const_spec((D, 2 * dff)), _const_spec((FFN_CONV, 2 * dff)), _const_spec((1, 2 * dff)),
                  _const_spec((dff, D)), _const_spec((1, D))],
        out_specs=ftile(D),
        out_shape=jax.ShapeDtypeStruct((B, S, D), x.dtype),
        scratch_shapes=[pltpu.VMEM((HALO, 2 * dff), F32)]
        + [pltpu.VMEM((2, HALO + tf, FFN_COL_CHUNK), F32) for _ in range(FFN_STAGES)],
        compiler_params=params,
    )(x, att, mh, w_out[0].astype(BF16), norm_ffn_g[0].reshape(1, D), w_up[0].astype(BF16), ffn_conv_w[0],
      ffn_conv_b[0].reshape(1, 2 * dff), w_down[0].astype(BF16), norm_final_g.reshape(1, D))
    return out
```
